```python
import jax
import jax.numpy as jnp
from jax import lax
import numpy as np

D_MODEL = 1024
BATCH = 8
SEQ = 4096
DEPTH = 2

GRID_W = 64
CTX_LEN = 256
MIX_W = D_MODEL
HEAD_DIM = 64
N_HEADS = MIX_W // HEAD_DIM
W_LORA = 64
A_LORA = 64
G_LORA = 128
LNX_EPS = 64e-5
LRU_BLOCKS = N_HEADS
LRU_BLOCK_W = MIX_W // LRU_BLOCKS
LRU_C = 8.0
LRU_CONV = 4
NA_KH = 8
NA_KW = 16
NA_QROWS = 2
NA_QCOLS = 16
NA_KCOLS = NA_QCOLS + NA_KW
ROPE_BASE = 10000.0
D_FF = 2816
FFN_CONV = 3
EPS = 1e-6
NEG_INF = -1e30
A_COLS = 3 * MIX_W + W_LORA + A_LORA + G_LORA
B_COLS = 2 * MIX_W
C_COLS = 3 * MIX_W
G_COLS = 3 * D_MODEL
N_IN = A_COLS + B_COLS + C_COLS + G_COLS

kernel_name = 'hybrid_rwkv7_rglru_natten_dit'


def rms_norm(x, g):
    xf = x.astype(jnp.float32)
    y = xf * lax.rsqrt(jnp.mean(xf * xf, axis=-1, keepdims=True) + EPS)
    return (y * g.astype(jnp.float32)).astype(x.dtype)


def split_cols(u, sizes):
    return jnp.split(u, [int(s) for s in np.cumsum(sizes)[:-1]], axis=-1)


def dwconv(u, w, left):
    width, T = w.shape[0], u.shape[1]
    up = jnp.pad(u, ((0, 0), (left, width - 1 - left), (0, 0)))
    out = w[0] * up[:, 0:T]
    for j in range(1, width):
        out = out + w[j] * up[:, j:j + T]
    return out


def token_shift(u, mu_prev, mu_next):
    zero = jnp.zeros_like(u[:, :1])
    u_prev = jnp.concatenate([zero, u[:, :-1]], axis=1)
    u_next = jnp.concatenate([u[:, 1:], zero], axis=1)
    return u + mu_prev * (u_prev - u) + mu_next * (u_next - u)


def wkv_scan(state, r, decay, k, v, kk, a, reverse):
    def step(S, inp):
        r_t, w_t, k_t, v_t, kk_t, a_t = inp
        sa = jnp.einsum('bhvk,bhk->bhv', S, -kk_t)
        S = (S * w_t[:, :, None, :] + sa[..., None] * (kk_t * a_t)[:, :, None, :]
             + v_t[..., None] * k_t[:, :, None, :])
        return S, jnp.einsum('bhvk,bhk->bhv', S, r_t)
    xs = tuple(jnp.swapaxes(t.astype(jnp.float32), 0, 1) for t in (r, decay, k, v, kk, a))
    state, ys = lax.scan(step, state, xs, reverse=reverse)
    return jnp.swapaxes(ys, 0, 1), state


def rwkv7_prepare(u, p):
    B, T, _ = u.shape
    heads = lambda t: t.reshape(B, T, N_HEADS, HEAD_DIM)
    u = token_shift(u, p['rwkv_mu'][0], p['rwkv_mu'][1])
    r, k, v, w_lo, a_lo, g_lo = split_cols(u, [MIX_W, MIX_W, MIX_W, W_LORA, A_LORA, G_LORA])
    kk = heads(k * p['rwkv_k_k']).astype(jnp.float32)
    kk = kk / jnp.maximum(jnp.sqrt(jnp.sum(kk * kk, axis=-1, keepdims=True)), 1e-12)
    w_lo = jnp.tanh(w_lo)
    per_dir = []
    for d in range(2):
        w_log = -jax.nn.softplus(-(p['rwkv_w0'][d] + w_lo @ p['rwkv_w_up'][d])) - 0.5
        decay = jnp.exp(-jnp.exp(w_log.astype(jnp.float32)))
        a = jax.nn.sigmoid(p['rwkv_a0'][d] + a_lo @ p['rwkv_a_up'][d])
        k_d = k * (1.0 + (a - 1.0) * p['rwkv_k_a'])
        per_dir.append((heads(decay), heads(k_d), heads(a)))
    g = jax.nn.sigmoid(g_lo) @ p['rwkv_g_up']
    return heads(r), heads(v), kk, per_dir, g


def rwkv7_output(y, r, v, k_f, k_b, g, p):
    B, T = y.shape[:2]
    mean = jnp.mean(y, axis=-1, keepdims=True)
    var = jnp.mean(jnp.square(y - mean), axis=-1, keepdims=True)
    yn = ((y - mean) * lax.rsqrt(var + LNX_EPS)).reshape(B, T, MIX_W)
    yn = (yn * p['rwkv_lnx_w'] + p['rwkv_lnx_b']).astype(v.dtype)
    bonus = jnp.sum(r * (k_f + k_b) * p['rwkv_r_k'], axis=-1, keepdims=True) * v
    return (yn + bonus.reshape(B, T, MIX_W)) * g


def rwkv7_branch(u_lat, u_ctx, p):
    r_l, v_l, kk_l, dirs_l, g_l = rwkv7_prepare(u_lat, p)
    r_c, v_c, kk_c, dirs_c, g_c = rwkv7_prepare(u_ctx, p)
    B = u_lat.shape[0]
    ys_l, ys_c = [], []
    for d in range(2):
        rev = d == 1
        s0 = jnp.zeros((B, N_HEADS, HEAD_DIM, HEAD_DIM), jnp.float32)
        dec_c, k_c, a_c = dirs_c[d]
        y_c, s_ctx = wkv_scan(s0, r_c, dec_c, k_c, v_c, kk_c, a_c, rev)
        dec_l, k_l, a_l = dirs_l[d]
        y_l, _ = wkv_scan(s_ctx, r_l, dec_l, k_l, v_l, kk_l, a_l, rev)
        ys_c.append(y_c)
        ys_l.append(y_l)
    out_l = rwkv7_output(ys_l[0] + ys_l[1], r_l, v_l, dirs_l[0][1], dirs_l[1][1], g_l, p)
    out_c = rwkv7_output(ys_c[0] + ys_c[1], r_c, v_c, dirs_c[0][1], dirs_c[1][1], g_c, p)
    return out_l, out_c


def lin_combine(e1, e2):
    a1, b1 = e1
    a2, b2 = e2
    return a1 * a2, a2 * b1 + b2


def linear_scan(a, b, h0, reverse):
    a_cum, b_cum = lax.associative_scan(lin_combine, (a, b), axis=1, reverse=reverse)
    h = a_cum * h0[:, None, :] + b_cum
    return h, (h[:, 0] if reverse else h[:, -1])


def rglru_prepare(u, p):
    xb, gb = split_cols(u, [MIX_W, MIX_W])
    xb = dwconv(xb, p['lru_conv_w'], LRU_CONV // 2) + p['lru_conv_b']
    return xb, jax.nn.gelu(gb)


def rglru_coeffs(xb, p, d):
    B, T, _ = xb.shape
    xg = xb.reshape(B, T, LRU_BLOCKS, LRU_BLOCK_W)
    r = jax.nn.sigmoid(jnp.einsum('btgi,gij->btgj', xg, p['lru_gate_a_w'][d]).reshape(B, T, MIX_W)
                       + p['lru_gate_a_b'][d])
    i = jax.nn.sigmoid(jnp.einsum('btgi,gij->btgj', xg, p['lru_gate_x_w'][d]).reshape(B, T, MIX_W)
                       + p['lru_gate_x_b'][d])
    log_a = -LRU_C * r.astype(jnp.float32) * jax.nn.softplus(-p['lru_lambda'][d].astype(jnp.float32))
    a = jnp.exp(log_a)
    b = jnp.sqrt(1.0 - a * a) * (i * xb).astype(jnp.float32)
    return a, b


def rglru_branch(u_lat, u_ctx, p):
    x_l, gate_l = rglru_prepare(u_lat, p)
    x_c, gate_c = rglru_prepare(u_ctx, p)
    B = u_lat.shape[0]
    h_l, h_c = [], []
    for d in range(2):
        rev = d == 1
        a_c, b_c = rglru_coeffs(x_c, p, d)
        hc, h_end = linear_scan(a_c, b_c, jnp.zeros((B, MIX_W), jnp.float32), rev)
        a_l, b_l = rglru_coeffs(x_l, p, d)
        hl, _ = linear_scan(a_l, b_l, h_end, rev)
        h_c.append(hc)
        h_l.append(hl)
    y_l = (h_l[0] + h_l[1]).astype(u_lat.dtype) * gate_l
    y_c = (h_c[0] + h_c[1]).astype(u_ctx.dtype) * gate_c
    return y_l, y_c


def rope_2d(x, rows, cols):
    half = HEAD_DIM // 2
    freqs = ROPE_BASE ** (-jnp.arange(0, half, 2, dtype=jnp.float32) / half)

    def rot(xp, pos):
        ang = pos.astype(jnp.float32)[:, None] * freqs[None, :]
        cos = jnp.cos(ang)[:, None, :].astype(x.dtype)
        sin = jnp.sin(ang)[:, None, :].astype(x.dtype)
        x1, x2 = xp[..., :half // 2], xp[..., half // 2:]
        return jnp.concatenate([x1 * cos - x2 * sin, x1 * sin + x2 * cos], axis=-1)
    return jnp.concatenate([rot(x[..., :half], rows), rot(x[..., half:], cols)], axis=-1)


def neighbourhood_attention(q, k, v, q_plain, k_ctx, v_ctx, rpb):
    B, H, rows, W, dh = q.shape
    kh = min(NA_KH, rows)
    kbh = min(kh + NA_QROWS - 1, rows)
    ncb = W // NA_QCOLS
    nq = NA_QROWS * NA_QCOLS
    nloc = kbh * NA_KCOLS
    scale = dh ** -0.5
    qcol = np.arange(W).reshape(ncb, NA_QCOLS)
    kcol = (np.clip(np.arange(ncb) * NA_QCOLS - NA_KW // 2, 0, W - NA_KCOLS)[:, None]
            + np.arange(NA_KCOLS)[None, :])
    cs = np.clip(qcol - NA_KW // 2, 0, W - NA_KW)
    col_ok = (kcol[:, None, :] >= cs[:, :, None]) & (kcol[:, None, :] < cs[:, :, None] + NA_KW)
    dcol = np.clip(kcol[:, None, :] - qcol[:, :, None] + NA_KW - 1, 0, 2 * NA_KW - 2)

    def to_blocks(t):
        t = t.reshape(B, H, NA_QROWS, ncb, NA_QCOLS, dh).transpose(0, 1, 3, 2, 4, 5)
        return t.reshape(B, H, ncb, nq, dh)

    def band(t, kr):
        t = lax.dynamic_slice_in_dim(t, kr, kbh, axis=2)[:, :, :, kcol]
        return t.transpose(0, 1, 3, 2, 4, 5).reshape(B, H, ncb, nloc, dh)

    def row_block(blk):
        r0 = blk * NA_QROWS
        kr = jnp.clip(r0 - kh // 2, 0, rows - kbh)
        qb = to_blocks(lax.dynamic_slice_in_dim(q, r0, NA_QROWS, axis=2))
        qpb = to_blocks(lax.dynamic_slice_in_dim(q_plain, r0, NA_QROWS, axis=2))
        kb = band(k, kr)
        vb = band(v, kr)
        qrow = r0 + jnp.arange(NA_QROWS)
        krow = kr + jnp.arange(kbh)
        rs = jnp.clip(qrow - kh // 2, 0, rows - kh)
        row_ok = (krow[None, :] >= rs[:, None]) & (krow[None, :] < rs[:, None] + kh)
        drow = jnp.clip(krow[None, :] - qrow[:, None] + NA_KH - 1, 0, 2 * NA_KH - 2)
        ok = (row_ok[None, :, None, :, None] & col_ok[:, None, :, None, :]).reshape(ncb, nq, nloc)
        bias = rpb[:, drow[None, :, None, :, None], dcol[:, None, :, None, :]].reshape(H, ncb, nq, nloc)
        s_loc = jnp.einsum('bhnqd,bhnkd->bhnqk', qb, kb).astype(jnp.float32) * scale + bias.astype(jnp.float32)
        s_loc = jnp.where(ok, s_loc, NEG_INF)
        s_ctx = jnp.einsum('bhnqd,bhld->bhnql', qpb, k_ctx).astype(jnp.float32) * scale
        prob = jax.nn.softmax(jnp.concatenate([s_loc, s_ctx], axis=-1), axis=-1).astype(v.dtype)
        out = (jnp.einsum('bhnqk,bhnkd->bhnqd', prob[..., :nloc], vb)
               + jnp.einsum('bhnql,bhld->bhnqd', prob[..., nloc:], v_ctx))
        out = out.reshape(B, H, ncb, NA_QROWS, NA_QCOLS, dh).transpose(0, 1, 3, 2, 4, 5)
        return out.reshape(B, H, NA_QROWS, W, dh)

    out = lax.map(row_block, jnp.arange(rows // NA_QROWS))
    return out.transpose(1, 2, 0, 3, 4, 5).reshape(B, H, rows, W, dh)


def na_branch(u_lat, u_ctx, p, with_ctx):
    B, T, _ = u_lat.shape
    rows = T // GRID_W
    L = u_ctx.shape[1]
    q, k, v = [t.reshape(B, T, N_HEADS, HEAD_DIM) for t in split_cols(u_lat, [MIX_W] * 3)]
    qc, kc, vc = [t.reshape(B, L, N_HEADS, HEAD_DIM).transpose(0, 2, 1, 3)
                  for t in split_cols(u_ctx, [MIX_W] * 3)]
    pos = jnp.arange(T)
    prow, pcol = pos // GRID_W, pos % GRID_W
    grid = lambda t: t.reshape(B, rows, GRID_W, N_HEADS, HEAD_DIM).transpose(0, 3, 1, 2, 4)
    y = neighbourhood_attention(grid(rope_2d(q, prow, pcol)), grid(rope_2d(k, prow, pcol)), grid(v),
                                grid(q), kc, vc, p['na_rpb'])
    y_lat = y.transpose(0, 2, 3, 1, 4).reshape(B, T, MIX_W)
    if not with_ctx:
        return y_lat, None
    s = jnp.einsum('bhqd,bhkd->bhqk', qc, kc).astype(jnp.float32) * HEAD_DIM ** -0.5
    prob = jax.nn.softmax(s, axis=-1).astype(vc.dtype)
    y_ctx = jnp.einsum('bhqk,bhkd->bhqd', prob, vc).transpose(0, 2, 1, 3).reshape(B, L, MIX_W)
    return y_lat, y_ctx


def merge_branches(u_gate, y_a, y_b, y_c, p):
    g_a, g_b, g_c = jnp.split(jax.nn.sigmoid(u_gate), 3, axis=-1)
    w_br = p['w_branch']
    m = g_a * (y_a @ w_br[0]) + g_b * (y_b @ w_br[1]) + g_c * (y_c @ w_br[2])
    return m @ p['w_out']


def conv_ffn(h, p):
    gate, up = jnp.split(h @ p['ffn_w_in'], 2, axis=-1)
    gate = dwconv(gate, p['ffn_conv_w'], FFN_CONV // 2) + p['ffn_conv_b']
    return (jax.nn.silu(gate) * up) @ p['ffn_w_out']


def hybrid_layer(x, xc, mod, mod_c, p, update_ctx):
    sh1, sc1, g1, sh2, sc2, g2 = jnp.split(mod, 6, axis=-1)
    sh1c, sc1c, g1c, sh2c, sc2c, g2c = jnp.split(mod_c, 6, axis=-1)
    ng = p['norm_g']
    h = rms_norm(x, ng[0]) * (1.0 + sc1) + sh1
    hc = rms_norm(xc, ng[0]) * (1.0 + sc1c) + sh1c
    u_a, u_b, u_c, u_g = split_cols(h @ p['w_in'], [A_COLS, B_COLS, C_COLS, G_COLS])
    u_ac, u_bc, u_cc, u_gc = split_cols(hc @ p['w_in'], [A_COLS, B_COLS, C_COLS, G_COLS])
    y_a, y_ac = rwkv7_branch(u_a, u_ac, p)
    y_b, y_bc = rglru_branch(u_b, u_bc, p)
    y_c, y_cc = na_branch(u_c, u_cc, p, update_ctx)
    x = x + g1 * rms_norm(merge_branches(u_g, y_a, y_b, y_c, p), ng[1])
    h = rms_norm(x, ng[2]) * (1.0 + sc2) + sh2
    x = x + g2 * rms_norm(conv_ffn(h, p), ng[3])
    if update_ctx:
        xc = xc + g1c * rms_norm(merge_branches(u_gc, y_ac, y_bc, y_cc, p), ng[1])
        hc = rms_norm(xc, ng[2]) * (1.0 + sc2c) + sh2c
        xc = xc + g2c * rms_norm(conv_ffn(hc, p), ng[3])
    return x, xc


def setup_inputs(seed: int = 0) -> dict:
    key = jax.random.key(seed)
    ks = iter(jax.random.split(key, 48))
    f32 = jnp.float32
    L = DEPTH

    def nrm(shape, scale):
        return jax.random.normal(next(ks), shape, f32) * scale

    def unif(shape, lo, hi):
        return jax.random.uniform(next(ks), shape, f32, lo, hi)

    lam_u = unif((L, 2, MIX_W), 0.9, 0.999)
    return {
        'x': nrm((BATCH, SEQ, D_MODEL), 1.0),
        'c': nrm((BATCH, D_MODEL), 1.0),
        'ctx': nrm((BATCH, CTX_LEN, D_MODEL), 1.0),
        'c_ctx': nrm((D_MODEL,), 1.0),
        'ada_w': nrm((L, D_MODEL, 6 * D_MODEL), 0.5 * D_MODEL ** -0.5),
        'ada_b': nrm((L, 6 * D_MODEL), 0.01),
        'norm_g': 1.0 + nrm((L, 4, D_MODEL), 0.05),
        'w_in': nrm((L, D_MODEL, N_IN), D_MODEL ** -0.5),
        'rwkv_mu': unif((L, 2, A_COLS), 0.0, 0.45),
        'rwkv_w0': unif((L, 2, MIX_W), -5.0, 1.0),
        'rwkv_w_up': nrm((L, 2, W_LORA, MIX_W), 0.5 * W_LORA ** -0.5),
        'rwkv_a0': nrm((L, 2, MIX_W), 0.1),
        'rwkv_a_up': nrm((L, 2, A_LORA, MIX_W), 0.5 * A_LORA ** -0.5),
        'rwkv_g_up': nrm((L, G_LORA, MIX_W), G_LORA ** -0.5),
        'rwkv_k_k': 0.85 + nrm((L, MIX_W), 0.05),
        'rwkv_k_a': 1.0 + nrm((L, MIX_W), 0.05),
        'rwkv_r_k': nrm((L, N_HEADS, HEAD_DIM), 0.1),
        'rwkv_lnx_w': 1.0 + nrm((L, MIX_W), 0.05),
        'rwkv_lnx_b': nrm((L, MIX_W), 0.01),
        'lru_conv_w': nrm((L, LRU_CONV, MIX_W), 0.5),
        'lru_conv_b': nrm((L, MIX_W), 0.01),
        'lru_gate_a_w': nrm((L, 2, LRU_BLOCKS, LRU_BLOCK_W, LRU_BLOCK_W), LRU_BLOCK_W ** -0.5),
        'lru_gate_a_b': nrm((L, 2, MIX_W), 0.01),
        'lru_gate_x_w': nrm((L, 2, LRU_BLOCKS, LRU_BLOCK_W, LRU_BLOCK_W), LRU_BLOCK_W ** -0.5),
        'lru_gate_x_b': nrm((L, 2, MIX_W), 0.01),
        'lru_lambda': jnp.log(lam_u) - jnp.log1p(-lam_u),
        'na_rpb': nrm((L, N_HEADS, 2 * NA_KH - 1, 2 * NA_KW - 1), 0.1),
        'w_branch': nrm((L, 3, MIX_W, D_MODEL), MIX_W ** -0.5),
        'w_out': nrm((L, D_MODEL, D_MODEL), D_MODEL ** -0.5),
        'ffn_w_in': nrm((L, D_MODEL, 2 * D_FF), D_MODEL ** -0.5),
        'ffn_conv_w': nrm((L, FFN_CONV, D_FF), FFN_CONV ** -0.5),
        'ffn_conv_b': nrm((L, D_FF), 0.01),
        'ffn_w_out': nrm((L, D_FF, D_MODEL), D_FF ** -0.5),
    }


def reference(x, c, ctx, c_ctx, ada_w, ada_b, norm_g, w_in, rwkv_mu, rwkv_w0, rwkv_w_up, rwkv_a0,
              rwkv_a_up, rwkv_g_up, rwkv_k_k, rwkv_k_a, rwkv_r_k, rwkv_lnx_w, rwkv_lnx_b, lru_conv_w,
              lru_conv_b, lru_gate_a_w, lru_gate_a_b, lru_gate_x_w, lru_gate_x_b, lru_lambda, na_rpb,
              w_branch, w_out, ffn_w_in, ffn_conv_w, ffn_conv_b, ffn_w_out):
    xc = ctx
    for l in range(DEPTH):
        p = {
            'norm_g': norm_g[l], 'w_in': w_in[l],
            'rwkv_mu': rwkv_mu[l], 'rwkv_w0': rwkv_w0[l], 'rwkv_w_up': rwkv_w_up[l],
            'rwkv_a0': rwkv_a0[l], 'rwkv_a_up': rwkv_a_up[l], 'rwkv_g_up': rwkv_g_up[l],
            'rwkv_k_k': rwkv_k_k[l], 'rwkv_k_a': rwkv_k_a[l], 'rwkv_r_k': rwkv_r_k[l],
            'rwkv_lnx_w': rwkv_lnx_w[l], 'rwkv_lnx_b': rwkv_lnx_b[l],
            'lru_conv_w': lru_conv_w[l], 'lru_conv_b': lru_conv_b[l],
            'lru_gate_a_w': lru_gate_a_w[l], 'lru_gate_a_b': lru_gate_a_b[l],
            'lru_gate_x_w': lru_gate_x_w[l], 'lru_gate_x_b': lru_gate_x_b[l],
            'lru_lambda': lru_lambda[l], 'na_rpb': na_rpb[l],
            'w_branch': w_branch[l], 'w_out': w_out[l],
            'ffn_w_in': ffn_w_in[l], 'ffn_conv_w': ffn_conv_w[l], 'ffn_conv_b': ffn_conv_b[l],
            'ffn_w_out': ffn_w_out[l],
        }
        mod = (jax.nn.silu(c) @ ada_w[l] + ada_b[l])[:, None, :]
        mod_c = jax.nn.silu(c_ctx) @ ada_w[l] + ada_b[l]
        x, xc = hybrid_layer(x, xc, mod, mod_c, p, l < DEPTH - 1)
    return x
```

```python
import functools

import numpy as np
import jax
import jax.numpy as jnp
from jax import lax
from jax.experimental import pallas as pl
from jax.experimental.pallas import tpu as pltpu

F32 = jnp.float32
BF16 = jnp.bfloat16

GRID_W = 64
HEAD_DIM = 64
W_LORA = 64
A_LORA = 64
G_LORA = 128
LNX_EPS = 64e-5
LRU_C = 8.0
NA_KH = 8
NA_KW = 16
NA_QROWS = 2
NA_BAND_ROWS = 10
ROPE_BASE = 10000.0
EPS = 1e-6
NEG_INF = -1e30

V7X_LANES = 128
V7X_SUBLANES = 8
V7X_VMEM_LIMIT_BYTES = 56 * 1024 * 1024

PAIR_W = 2 * HEAD_DIM
WKV_CHUNK = 64
LRU_CHUNK = 128
ROW_TILE = 256
LRU_GROUP = 256
FFN_HALO = 16


def _cparams(sem):
    return pltpu.CompilerParams(dimension_semantics=sem, vmem_limit_bytes=V7X_VMEM_LIMIT_BYTES)


def _split_bf16(x, parts):
    out = []
    rem = x
    for _ in range(parts - 1):
        hi = rem.astype(BF16)
        out.append(hi)
        rem = rem - hi.astype(F32)
    out.append(rem.astype(BF16))
    return out


def _dot(a, b):
    return jnp.dot(a, b, preferred_element_type=F32)


def _dot_nt(a, b):
    return lax.dot_general(a, b, (((1,), (1,)), ((), ())), preferred_element_type=F32)


def _dot_tn(a, b):
    return lax.dot_general(a, b, (((0,), (0,)), ((), ())), preferred_element_type=F32)


def _dot_exact_lhs(a_bf16, x, parts):
    acc = None
    for piece in _split_bf16(x, parts):
        t = _dot(a_bf16, piece)
        acc = t if acc is None else acc + t
    return acc


def _dot_exact_rhs(x, b_bf16, parts):
    acc = None
    for piece in _split_bf16(x, parts):
        t = _dot(piece, b_bf16)
        acc = t if acc is None else acc + t
    return acc


def _sigmoid(x):
    return 1.0 / (1.0 + jnp.exp(-x))


def _softplus(x):
    return jnp.maximum(x, 0.0) + jnp.log(1.0 + jnp.exp(-jnp.abs(x)))


def _rms(x, g):
    return x * lax.rsqrt(jnp.mean(x * x, axis=-1, keepdims=True) + EPS) * g


def _mod_kernel(c_ref, w_ref, b_ref, o_ref):
    c = c_ref[...]
    s = c * _sigmoid(c)
    o_ref[0] = _dot(s.astype(BF16), w_ref[0].astype(BF16)) + b_ref[0]


def _modulation(cc, ada_w, ada_b):
    depth, d, n = ada_w.shape
    rows = cc.shape[0]
    tn = n // 4
    return pl.pallas_call(
        _mod_kernel,
        grid=(depth, n // tn),
        in_specs=[
            pl.BlockSpec((rows, d), lambda l, j: (0, 0)),
            pl.BlockSpec((1, d, tn), lambda l, j: (l, 0, j)),
            pl.BlockSpec((1, 1, tn), lambda l, j: (l, 0, j)),
        ],
        out_specs=pl.BlockSpec((1, rows, tn), lambda l, j: (l, 0, j)),
        out_shape=jax.ShapeDtypeStruct((depth, rows, n), F32),
        compiler_params=_cparams(("parallel", "parallel")),
        name="modulation",
    )(cc, ada_w, ada_b.reshape(depth, 1, n))


def _in_kernel(x_ref, mod_ref, g_ref, w_ref, o_ref, hb_ref, *, lc, tm):
    @pl.when(pl.program_id(2) == 0)
    def _():
        y = _rms(x_ref[0], g_ref[...])
        row = pl.program_id(1) * tm + lax.broadcasted_iota(jnp.int32, (tm, 1), 0)
        is_ctx = row < lc
        sh = jnp.where(is_ctx, mod_ref[0, 0, 0:1, :], mod_ref[0, 1, 0:1, :])
        sc = jnp.where(is_ctx, mod_ref[0, 0, 1:2, :], mod_ref[0, 1, 1:2, :])
        hb_ref[...] = (y * (1.0 + sc) + sh).astype(BF16)

    o_ref[0] = _dot(hb_ref[...], w_ref[...])


def _in_proj(xs, mod, g, w, lc, tm, tn):
    b, r, d = xs.shape
    n = w.shape[1]
    return pl.pallas_call(
        functools.partial(_in_kernel, lc=lc, tm=tm),
        grid=(b, r // tm, n // tn),
        in_specs=[
            pl.BlockSpec((1, tm, d), lambda i, t, j: (i, t, 0)),
            pl.BlockSpec((1, 2, 6, d), lambda i, t, j: (i, 0, 0, 0)),
            pl.BlockSpec((1, d), lambda i, t, j: (0, 0)),
            pl.BlockSpec((d, tn), lambda i, t, j: (0, j)),
        ],
        out_specs=pl.BlockSpec((1, tm, tn), lambda i, t, j: (i, t, j)),
        out_shape=jax.ShapeDtypeStruct((b, r, n), F32),
        scratch_shapes=[pltpu.VMEM((tm, d), BF16)],
        compiler_params=_cparams(("parallel", "parallel", "arbitrary")),
        name="in_proj",
    )(xs, mod, g, w)


def _stack_pair(x, lane_a):
    return jnp.concatenate([jnp.where(lane_a, x, 0.0), jnp.where(lane_a, 0.0, x)], axis=0)


def _wkv_pair(r, kd, v, kkn, a, lw, cum, s_prev, mask_s, mask_i, c):
    n = 2 * c
    lane_a = lax.broadcasted_iota(jnp.int32, (c, PAIR_W), 1) < HEAD_DIM
    tot = jnp.sum(lw, axis=0, keepdims=True)
    e_inv = jnp.exp(-cum)
    e_rem = jnp.exp(tot - cum)
    akk = a * kkn
    stack = lambda x: _stack_pair(x, lane_a).astype(BF16)
    al = stack(-kkn * jnp.exp(cum - lw))
    rt = stack(r * jnp.exp(cum))
    bh = stack(akk * e_inv)
    kh = stack(kd * e_inv)
    bc = stack(akk * e_rem)
    kc = stack(kd * e_rem)
    vs = stack(v)

    lhs = jnp.concatenate([al, rt], axis=0)
    g = _dot_nt(lhs, jnp.concatenate([bh, kh], axis=0))
    m_b = jnp.where(mask_s, g[:n, :n], 0.0)
    m_k = jnp.where(mask_s, g[:n, n:], 0.0)
    n_b = jnp.where(mask_i, g[n:, :n], 0.0)
    n_k = jnp.where(mask_i, g[n:, n:], 0.0)

    eye = (lax.broadcasted_iota(jnp.int32, (n, n), 0) == lax.broadcasted_iota(jnp.int32, (n, n), 1))
    t_inv = jnp.where(eye, 1.0, 0.0) + m_b
    m_pow = m_b
    span = 1
    while 2 * span < c:
        mp = m_pow.astype(BF16)
        m_pow = _dot(mp, mp)
        t_inv = t_inv + _dot(t_inv.astype(BF16), m_pow.astype(BF16))
        span *= 2

    ls = _dot_nt(lhs, s_prev.astype(BF16))
    rhs = ls[:n] + _dot(m_k.astype(BF16), vs)
    u = _dot(t_inv.astype(BF16), rhs.astype(BF16)).astype(BF16)
    uv = jnp.concatenate([u, vs], axis=0)
    ys = ls[n:] + _dot(jnp.concatenate([n_b.astype(BF16), n_k.astype(BF16)], axis=1), uv)
    y = ys[:c] + ys[c:]
    s_new = s_prev * jnp.exp(tot) + _dot_tn(uv, jnp.concatenate([bc, kc], axis=0))
    return y, s_new


def _wkv_kernel(uf_ref, ufp_ref, ufn_ref, ub_ref, ubp_ref, ubn_ref,
                mu_ref, w0_ref, wup_ref, a0_ref, aup_ref, gup_ref, kk_ref, ka_ref, rk_ref,
                tri_ref, ms_ref, mi_ref, bd_ref,
                yf_ref, yb_ref, eb_ref, eg_ref,
                s_ref, stg_ref, rkk_ref, yst_ref, bst_ref, *, c, nc, nl):
    i = pl.program_id(1)
    width = yf_ref.shape[2]
    npair = width // PAIR_W
    cb = jnp.where(i < nc, nc - 1 - i, 2 * nc + nl - 1 - i)

    @pl.when(i == 0)
    def _():
        s_ref[...] = jnp.zeros(s_ref.shape, F32)

    rows = lax.broadcasted_iota(jnp.int32, (c, 1), 0)

    def shifted(u_ref, p_ref, n_ref, ci):
        u = u_ref[0]
        first = jnp.logical_or(ci == 0, ci == nc)
        last = jnp.logical_or(ci == nc - 1, ci == nc + nl - 1)
        prev_row = jnp.where(first, 0.0, p_ref[0, V7X_SUBLANES - 1:V7X_SUBLANES, :])
        next_row = jnp.where(last, 0.0, n_ref[0, 0:1, :])
        u_prev = jnp.where(rows == 0, prev_row, pltpu.roll(u, 1, 0))
        u_next = jnp.where(rows == c - 1, next_row, pltpu.roll(u, c - 1, 0))
        return u + mu_ref[0:1, :] * (u_prev - u) + mu_ref[1:2, :] * (u_next - u)

    def lora_a(a_lo, d):
        return _sigmoid(a0_ref[d:d + 1, :] + _dot(a_lo, aup_ref[d]))

    def prepare(us, d):
        r = us[:, 0:width]
        k = us[:, width:2 * width]
        v = us[:, 2 * width:3 * width]
        o = 3 * width
        w_lo = jnp.tanh(us[:, o:o + W_LORA]).astype(BF16)
        a_lo = us[:, o + W_LORA:o + W_LORA + A_LORA].astype(BF16)
        wpre = w0_ref[d:d + 1, :] + _dot(w_lo, wup_ref[d])
        lw = -jnp.exp(-_softplus(-wpre) - 0.5)
        a = lora_a(a_lo, d)
        kd = k * (1.0 + (a - 1.0) * ka_ref[...])
        kkraw = k * kk_ref[...]
        cum = _dot_exact_lhs(tri_ref[d], lw, 3)
        for q, arr in enumerate((r, kd, v, kkraw, a, lw, cum)):
            for p in range(npair):
                stg_ref[d, q, p] = arr[:, p * PAIR_W:(p + 1) * PAIR_W]
        return r, k, kd, a_lo

    us_f = shifted(uf_ref, ufp_ref, ufn_ref, i)
    r_f, k_f, kd_f, a_lo_f = prepare(us_f, 0)
    a_rev = lora_a(a_lo_f, 1)
    kd_rev = k_f * (1.0 + (a_rev - 1.0) * ka_ref[...])
    rkk = r_f * (kd_f + kd_rev) * rk_ref[...]
    for p in range(npair):
        rkk_ref[p] = rkk[:, p * PAIR_W:(p + 1) * PAIR_W]
    o = 3 * width + W_LORA + A_LORA
    g_lo = _sigmoid(us_f[:, o:o + G_LORA]).astype(BF16)
    eg_ref[0] = _dot(g_lo, gup_ref[...])

    us_b = shifted(ub_ref, ubp_ref, ubn_ref, cb)
    prepare(us_b, 1)

    bd = bd_ref[...]

    def pair_body(p, carry):
        for d in range(2):
            r, kd, v, kkraw, a, lw, cum = [stg_ref[d, q, p] for q in range(7)]
            ss = _dot_exact_rhs(kkraw * kkraw, bd, 2)
            kkn = kkraw * lax.rsqrt(jnp.maximum(ss, 1e-24))
            y, s_new = _wkv_pair(r, kd, v, kkn, a, lw, cum, s_ref[d, p],
                                 ms_ref[d] > 0.5, mi_ref[d] > 0.5, c)
            s_ref[d, p] = s_new
            yst_ref[d, p] = y
        bst_ref[p] = _dot_exact_rhs(rkk_ref[p], bd, 2) * stg_ref[0, 2, p]
        return carry

    lax.fori_loop(0, npair, pair_body, 0)

    for p in range(npair):
        sl = slice(p * PAIR_W, (p + 1) * PAIR_W)
        yf_ref[0, :, sl] = yst_ref[0, p]
        yb_ref[0, :, sl] = yst_ref[1, p]
        eb_ref[0, :, sl] = bst_ref[p]


def _wkv_consts(c):
    t = np.arange(c)
    lower = (t[:, None] >= t[None, :])
    tri = np.stack([lower, lower.T]).astype(np.float32)
    strict = np.stack([t[:, None] > t[None, :], t[:, None] < t[None, :]])
    incl = np.stack([lower, lower.T])
    ms = np.tile(strict, (1, 2, 2)).astype(np.float32)
    mi = np.tile(incl, (1, 2, 2)).astype(np.float32)
    h = np.arange(PAIR_W) // HEAD_DIM
    bd = (h[:, None] == h[None, :]).astype(np.float32)
    return jnp.asarray(tri, BF16), jnp.asarray(ms), jnp.asarray(mi), jnp.asarray(bd, BF16)


def _wkv(u_a, p, lc):
    b, r, fa = u_a.shape
    width = p["rwkv_k_k"].shape[0]
    c = WKV_CHUNK
    nc, nl = lc // c, (r - lc) // c
    nchunk = nc + nl
    hb = c // V7X_SUBLANES
    nhalo = r // V7X_SUBLANES
    npair = width // PAIR_W
    tri, ms, mi, bd = _wkv_consts(c)

    def rev(i):
        return jnp.where(i < nc, nc - 1 - i, 2 * nc + nl - 1 - i)

    cur_f = lambda bi, i: (bi, i, 0)
    prev_f = lambda bi, i: (bi, jnp.maximum(i * hb - 1, 0), 0)
    next_f = lambda bi, i: (bi, jnp.minimum((i + 1) * hb, nhalo - 1), 0)
    cur_b = lambda bi, i: (bi, rev(i), 0)
    prev_b = lambda bi, i: (bi, jnp.maximum(rev(i) * hb - 1, 0), 0)
    next_b = lambda bi, i: (bi, jnp.minimum((rev(i) + 1) * hb, nhalo - 1), 0)
    const2 = lambda bi, i: (0, 0)
    const3 = lambda bi, i: (0, 0, 0)

    row = lambda x: x.reshape(1, -1)
    out_sd = jax.ShapeDtypeStruct((b, r, width), F32)
    return pl.pallas_call(
        functools.partial(_wkv_kernel, c=c, nc=nc, nl=nl),
        grid=(b, nchunk),
        in_specs=[
            pl.BlockSpec((1, c, fa), cur_f),
            pl.BlockSpec((1, V7X_SUBLANES, fa), prev_f),
            pl.BlockSpec((1, V7X_SUBLANES, fa), next_f),
            pl.BlockSpec((1, c, fa), cur_b),
            pl.BlockSpec((1, V7X_SUBLANES, fa), prev_b),
            pl.BlockSpec((1, V7X_SUBLANES, fa), next_b),
            pl.BlockSpec((2, fa), const2),
            pl.BlockSpec((2, width), const2),
            pl.BlockSpec((2, W_LORA, width), const3),
            pl.BlockSpec((2, width), const2),
            pl.BlockSpec((2, A_LORA, width), const3),
            pl.BlockSpec((G_LORA, width), const2),
            pl.BlockSpec((1, width), const2),
            pl.BlockSpec((1, width), const2),
            pl.BlockSpec((1, width), const2),
            pl.BlockSpec((2, c, c), const3),
            pl.BlockSpec((2, 2 * c, 2 * c), const3),
            pl.BlockSpec((2, 2 * c, 2 * c), const3),
            pl.BlockSpec((PAIR_W, PAIR_W), const2),
        ],
        out_specs=[
            pl.BlockSpec((1, c, width), cur_f),
            pl.BlockSpec((1, c, width), cur_b),
            pl.BlockSpec((1, c, width), cur_f),
            pl.BlockSpec((1, c, width), cur_f),
        ],
        out_shape=[out_sd, out_sd, out_sd, out_sd],
        scratch_shapes=[
            pltpu.VMEM((2, npair, PAIR_W, PAIR_W), F32),
            pltpu.VMEM((2, 7, npair, c, PAIR_W), F32),
            pltpu.VMEM((npair, c, PAIR_W), F32),
            pltpu.VMEM((2, npair, c, PAIR_W), F32),
            pltpu.VMEM((npair, c, PAIR_W), F32),
        ],
        compiler_params=_cparams(("parallel", "arbitrary")),
        name="wkv7",
    )(u_a, u_a, u_a, u_a, u_a, u_a,
      p["rwkv_mu"], p["rwkv_w0"], p["rwkv_w_up"].astype(BF16), p["rwkv_a0"],
      p["rwkv_a_up"].astype(BF16), p["rwkv_g_up"].astype(BF16),
      row(p["rwkv_k_k"]), row(p["rwkv_k_a"]), row(p["rwkv_r_k"]),
      tri, ms, mi, bd)


def _lru_scan(a, bv, h0, c, reverse):
    rows = lax.broadcasted_iota(jnp.int32, (c, 1), 0)
    s = 1
    while s < c:
        if reverse:
            keep = rows < c - s
            a_sh = jnp.where(keep, pltpu.roll(a, c - s, 0), 1.0)
            b_sh = jnp.where(keep, pltpu.roll(bv, c - s, 0), 0.0)
        else:
            keep = rows >= s
            a_sh = jnp.where(keep, pltpu.roll(a, s, 0), 1.0)
            b_sh = jnp.where(keep, pltpu.roll(bv, s, 0), 0.0)
        bv = a * b_sh + bv
        a = a * a_sh
        s *= 2
    return bv + a * h0


def _lru_kernel(uf_ref, ufp_ref, ufn_ref, ub_ref, ubp_ref, ubn_ref,
                cw_ref, cbias_ref, wa_ref, ba_ref, wx_ref, bx_ref, lam_ref,
                hf_ref, hb_ref, h_ref, *, c, nc, nl):
    i = pl.program_id(1)
    width = hf_ref.shape[2]
    cb = jnp.where(i < nc, nc - 1 - i, 2 * nc + nl - 1 - i)

    @pl.when(i == 0)
    def _():
        h_ref[...] = jnp.zeros(h_ref.shape, F32)

    rows = lax.broadcasted_iota(jnp.int32, (c, 1), 0)

    def coeffs(u_ref, p_ref, n_ref, ci, d):
        xb = u_ref[0]
        first = jnp.logical_or(ci == 0, ci == nc)
        last = jnp.logical_or(ci == nc - 1, ci == nc + nl - 1)
        p6 = jnp.where(first, 0.0, p_ref[0, V7X_SUBLANES - 2:V7X_SUBLANES - 1, :])
        p7 = jnp.where(first, 0.0, p_ref[0, V7X_SUBLANES - 1:V7X_SUBLANES, :])
        n0 = jnp.where(last, 0.0, n_ref[0, 0:1, :])
        x_m2 = jnp.where(rows == 0, p6, jnp.where(rows == 1, p7, pltpu.roll(xb, 2, 0)))
        x_m1 = jnp.where(rows == 0, p7, pltpu.roll(xb, 1, 0))
        x_p1 = jnp.where(rows == c - 1, n0, pltpu.roll(xb, c - 1, 0))
        xc = (cw_ref[0:1, :] * x_m2 + cw_ref[1:2, :] * x_m1 + cw_ref[2:3, :] * xb
              + cw_ref[3:4, :] * x_p1 + cbias_ref[...])
        xcb = xc.astype(BF16)
        ng = width // LRU_GROUP
        ga = jnp.concatenate([_dot(xcb[:, g * LRU_GROUP:(g + 1) * LRU_GROUP], wa_ref[d, g])
                              for g in range(ng)], axis=1)
        gx = jnp.concatenate([_dot(xcb[:, g * LRU_GROUP:(g + 1) * LRU_GROUP], wx_ref[d, g])
                              for g in range(ng)], axis=1)
        rg = _sigmoid(ga + ba_ref[d:d + 1, :])
        ig = _sigmoid(gx + bx_ref[d:d + 1, :])
        a = jnp.exp(-LRU_C * rg * _softplus(-lam_ref[d:d + 1, :]))
        bv = jnp.sqrt(1.0 - a * a) * (ig * xc)
        return a, bv

    a_f, b_f = coeffs(uf_ref, ufp_ref, ufn_ref, i, 0)
    h_f = _lru_scan(a_f, b_f, h_ref[0:1, :], c, False)
    h_ref[0:1, :] = h_f[c - 1:c, :]
    hf_ref[0] = h_f

    a_b, b_b = coeffs(ub_ref, ubp_ref, ubn_ref, cb, 1)
    h_b = _lru_scan(a_b, b_b, h_ref[1:2, :], c, True)
    h_ref[1:2, :] = h_b[0:1, :]
    hb_ref[0] = h_b


def _block_diag_groups(w):
    nd, nb, bw, _ = w.shape
    per = LRU_GROUP // bw
    w = w.reshape(nd, nb // per, per, bw, bw)
    eye = jnp.eye(per, dtype=w.dtype)
    out = jnp.einsum("dgpij,pq->dgpiqj", w, eye)
    return out.reshape(nd, nb // per, LRU_GROUP, LRU_GROUP).astype(BF16)


def _lru(u_b, p, lc):
    b, r, fb = u_b.shape
    width = fb // 2
    c = LRU_CHUNK
    nc, nl = lc // c, (r - lc) // c
    nchunk = nc + nl
    hb = c // V7X_SUBLANES
    nhalo = r // V7X_SUBLANES
    ng = width // LRU_GROUP

    def rev(i):
        return jnp.where(i < nc, nc - 1 - i, 2 * nc + nl - 1 - i)

    cur_f = lambda bi, i: (bi, i, 0)
    prev_f = lambda bi, i: (bi, jnp.maximum(i * hb - 1, 0), 0)
    next_f = lambda bi, i: (bi, jnp.minimum((i + 1) * hb, nhalo - 1), 0)
    cur_b = lambda bi, i: (bi, rev(i), 0)
    prev_b = lambda bi, i: (bi, jnp.maximum(rev(i) * hb - 1, 0), 0)
    next_b = lambda bi, i: (bi, jnp.minimum((rev(i) + 1) * hb, nhalo - 1), 0)
    const2 = lambda bi, i: (0, 0)
    const4 = lambda bi, i: (0, 0, 0, 0)
    out_sd = jax.ShapeDtypeStruct((b, r, width), F32)
    return pl.pallas_call(
        functools.partial(_lru_kernel, c=c, nc=nc, nl=nl),
        grid=(b, nchunk),
        in_specs=[
            pl.BlockSpec((1, c, width), cur_f),
            pl.BlockSpec((1, V7X_SUBLANES, width), prev_f),
            pl.BlockSpec((1, V7X_SUBLANES, width), next_f),
            pl.BlockSpec((1, c, width), cur_b),
            pl.BlockSpec((1, V7X_SUBLANES, width), prev_b),
            pl.BlockSpec((1, V7X_SUBLANES, width), next_b),
            pl.BlockSpec((4, width), const2),
            pl.BlockSpec((1, width), const2),
            pl.BlockSpec((2, ng, LRU_GROUP, LRU_GROUP), const4),
            pl.BlockSpec((2, width), const2),
            pl.BlockSpec((2, ng, LRU_GROUP, LRU_GROUP), const4),
            pl.BlockSpec((2, width), const2),
            pl.BlockSpec((2, width), const2),
        ],
        out_specs=[pl.BlockSpec((1, c, width), cur_f), pl.BlockSpec((1, c, width), cur_b)],
        out_shape=[out_sd, out_sd],
        scratch_shapes=[pltpu.VMEM((2, width), F32)],
        compiler_params=_cparams(("parallel", "arbitrary")),
        name="rglru",
    )(u_b, u_b, u_b, u_b, u_b, u_b,
      p["lru_conv_w"], p["lru_conv_b"].reshape(1, -1),
      _block_diag_groups(p["lru_gate_a_w"]), p["lru_gate_a_b"],
      _block_diag_groups(p["lru_gate_x_w"]), p["lru_gate_x_b"], p["lru_lambda"])


def _na_geometry(rows):
    assert rows >= NA_BAND_ROWS
    kh = min(NA_KH, rows)
    kbh = NA_BAND_ROWS
    deltas = []
    for blk in range(rows // NA_QROWS):
        r0 = blk * NA_QROWS
        kr = int(np.clip(r0 - kh // 2, 0, rows - kbh))
        deltas.append(r0 - kr)
    return kh, kbh, deltas, sorted(set(deltas))


def _na_bias_table(rpb, rows):
    w = GRID_W
    kh, kbh, deltas, geos = _na_geometry(rows)
    nq, nk = NA_QROWS * w, kbh * w
    drow = np.zeros((len(geos), nq, nk), np.int32)
    dcol = np.zeros((len(geos), nq, nk), np.int32)
    ok = np.zeros((len(geos), nq, nk), bool)
    qc = np.tile(np.arange(w), NA_QROWS)
    qr = np.repeat(np.arange(NA_QROWS), w)
    kc = np.tile(np.arange(w), kbh)
    kj = np.repeat(np.arange(kbh), w)
    cs = np.clip(qc - NA_KW // 2, 0, w - NA_KW)
    col_ok = (kc[None, :] >= cs[:, None]) & (kc[None, :] < cs[:, None] + NA_KW)
    dc = np.clip(kc[None, :] - qc[:, None] + NA_KW - 1, 0, 2 * NA_KW - 2)
    for gi, delta in enumerate(geos):
        seen = None
        for blk, dl in enumerate(deltas):
            if dl != delta:
                continue
            r0 = blk * NA_QROWS
            kr = r0 - delta
            qrow = r0 + qr
            krow = kr + kj
            rs = np.clip(qrow - kh // 2, 0, rows - kh)
            row_ok = (krow[None, :] >= rs[:, None]) & (krow[None, :] < rs[:, None] + kh)
            dr = np.clip(krow[None, :] - qrow[:, None] + NA_KH - 1, 0, 2 * NA_KH - 2)
            cur = (row_ok & col_ok, dr)
            if seen is None:
                seen = cur
            else:
                assert np.array_equal(seen[0], cur[0]) and np.array_equal(seen[1], cur[1])
        ok[gi], drow[gi], dcol[gi] = seen[0], seen[1], dc
    bias = rpb[:, drow, dcol]
    bias = jnp.where(ok[None], bias, NEG_INF)
    h = rpb.shape[0]
    bias = bias.reshape(h // 2, 2, len(geos), nq, nk).transpose(0, 2, 1, 3, 4)
    return bias, deltas, geos


def _rope_tables(t):
    half = HEAD_DIM // 2
    freqs = ROPE_BASE ** (-jnp.arange(0, half, 2, dtype=F32) / half)
    pos = jnp.arange(t)
    prow, pcol = pos // GRID_W, pos % GRID_W
    ang_r = prow.astype(F32)[:, None] * freqs[None, :]
    ang_c = pcol.astype(F32)[:, None] * freqs[None, :]
    cos = jnp.concatenate([jnp.cos(ang_r)] * 2 + [jnp.cos(ang_c)] * 2, axis=-1)
    sin = jnp.concatenate([-jnp.sin(ang_r), jnp.sin(ang_r), -jnp.sin(ang_c), jnp.sin(ang_c)], axis=-1)
    return jnp.tile(cos, (1, 2)), jnp.tile(sin, (1, 2))


def _na_kernel(q_ref, k_ref, v_ref, cos_ref, sin_ref, bias_ref, geo_ref, o_ref,
               qr_s, qp_s, kr_s, v_s, *, lc, rows, kbh, with_ctx):
    w = GRID_W
    t = rows * w
    nq, nk = NA_QROWS * w, kbh * w
    scale = HEAD_DIM ** -0.5
    quarter = HEAD_DIM // 4

    def rope(x):
        lane = lax.broadcasted_iota(jnp.int32, x.shape, 1)
        first = (lane % (2 * quarter)) < quarter
        swapped = jnp.where(first, pltpu.roll(x, PAIR_W - quarter, 1), pltpu.roll(x, quarter, 1))
        return x * cos_ref[...] + swapped * sin_ref[...]

    q_lat = q_ref[0, lc:, :]
    qr_s[...] = rope(q_lat).astype(BF16)
    qp_s[...] = q_ref[0].astype(BF16)
    kr_s[...] = rope(k_ref[0, lc:, :]).astype(BF16)
    v_s[...] = v_ref[0].astype(BF16)
    k_ctx = k_ref[0, :lc, :].astype(BF16)

    lane_a = lax.broadcasted_iota(jnp.int32, (nq, PAIR_W), 1) < HEAD_DIM

    def softmax_pv(parts):
        m = functools.reduce(jnp.maximum, [jnp.max(s, axis=-1, keepdims=True) for s, _ in parts])
        den = 0.0
        acc = 0.0
        for s, vals in parts:
            e = jnp.exp(s - m)
            den = den + jnp.sum(e, axis=-1, keepdims=True)
            acc = acc + _dot(e.astype(BF16), vals)
        return acc / den

    def block(blk, carry):
        r0 = blk * NA_QROWS
        kr = jnp.clip(r0 - min(NA_KH, rows) // 2, 0, rows - kbh)
        geo = geo_ref[blk]
        q0 = pl.multiple_of(r0 * w, w)
        k0 = pl.multiple_of(kr * w, w)
        qb = qr_s[pl.ds(q0, nq), :]
        qpb = qp_s[pl.ds(lc + q0, nq), :]
        kb = kr_s[pl.ds(k0, nk), :]
        vb = v_s[pl.ds(lc + k0, nk), :]
        v_ctx = v_s[0:lc, :]
        outs = []
        for hh in range(2):
            sel = lane_a if hh == 0 else jnp.logical_not(lane_a)
            zero = jnp.zeros((), BF16)
            s_loc = _dot_nt(jnp.where(sel, qb, zero), kb) * scale + bias_ref[0, geo, hh]
            s_ctx = _dot_nt(jnp.where(sel, qpb, zero), k_ctx) * scale
            outs.append(softmax_pv([(s_loc, vb), (s_ctx, v_ctx)]))
        o_ref[0, pl.ds(lc + q0, nq), :] = jnp.where(lane_a, outs[0], outs[1])
        return carry

    lax.fori_loop(0, rows // NA_QROWS, block, 0)

    if with_ctx:
        v_ctx = v_s[0:lc, :]
        for cblk in range(lc // nq):
            qcb = qp_s[cblk * nq:(cblk + 1) * nq, :]
            outs = []
            for hh in range(2):
                sel = lane_a if hh == 0 else jnp.logical_not(lane_a)
                s = _dot_nt(jnp.where(sel, qcb, jnp.zeros((), BF16)), k_ctx) * scale
                outs.append(softmax_pv([(s, v_ctx)]))
            o_ref[0, cblk * nq:(cblk + 1) * nq, :] = jnp.where(lane_a, outs[0], outs[1])
    else:
        o_ref[0, 0:lc, :] = jnp.zeros((lc, PAIR_W), F32)


def _na(u_c, rpb, lc, with_ctx):
    b, r, fc = u_c.shape
    width = fc // 3
    npair = width // PAIR_W
    t = r - lc
    rows = t // GRID_W
    bias, deltas, geos = _na_bias_table(rpb, rows)
    kh, kbh, _, _ = _na_geometry(rows)
    geo_idx = jnp.asarray([geos.index(d) for d in deltas], jnp.int32)
    cos, sin = _rope_tables(t)
    nq, nk = NA_QROWS * GRID_W, kbh * GRID_W
    return pl.pallas_call(
        functools.partial(_na_kernel, lc=lc, rows=rows, kbh=kbh, with_ctx=with_ctx),
        grid=(npair, b),
        in_specs=[
            pl.BlockSpec((1, r, PAIR_W), lambda p, i: (i, 0, p)),
            pl.BlockSpec((1, r, PAIR_W), lambda p, i: (i, 0, npair + p)),
            pl.BlockSpec((1, r, PAIR_W), lambda p, i: (i, 0, 2 * npair + p)),
            pl.BlockSpec((t, PAIR_W), lambda p, i: (0, 0)),
            pl.BlockSpec((t, PAIR_W), lambda p, i: (0, 0)),
            pl.BlockSpec((1, len(geos), 2, nq, nk), lambda p, i: (p, 0, 0, 0, 0)),
            pl.BlockSpec(memory_space=pltpu.SMEM),
        ],
        out_specs=pl.BlockSpec((1, r, PAIR_W), lambda p, i: (i, 0, p)),
        out_shape=jax.ShapeDtypeStruct((b, r, width), F32),
        scratch_shapes=[
            pltpu.VMEM((t, PAIR_W), BF16),
            pltpu.VMEM((r, PAIR_W), BF16),
            pltpu.VMEM((t, PAIR_W), BF16),
            pltpu.VMEM((r, PAIR_W), BF16),
        ],
        compiler_params=_cparams(("parallel", "parallel")),
        name="natten",
    )(u_c, u_c, u_c, cos, sin, bias, geo_idx)


def _merge_kernel(x_ref, ug_ref, yf_ref, yb_ref, eb_ref, eg_ref, hf_ref, hb_ref, gb_ref, yc_ref,
                  mod_ref, lw_ref, lb_ref, e_ref, et_ref, wbr_ref, wo_ref, g_ref, o_ref):
    width = yf_ref.shape[2]

    def group_mean(z):
        s = _dot_exact_rhs(z, e_ref[...], 2)
        return _dot_exact_rhs(s, et_ref[...], 2) * (1.0 / HEAD_DIM)

    y = yf_ref[0] + yb_ref[0]
    dev = y - group_mean(y)
    yn = dev * lax.rsqrt(group_mean(dev * dev) + LNX_EPS)
    y_a = (yn * lw_ref[...] + lb_ref[...] + eb_ref[0]) * eg_ref[0]
    y_b = (hf_ref[0] + hb_ref[0]) * jax.nn.gelu(gb_ref[0])
    ug = ug_ref[0]
    m = (_sigmoid(ug[:, 0:width]) * _dot(y_a.astype(BF16), wbr_ref[0])
         + _sigmoid(ug[:, width:2 * width]) * _dot(y_b.astype(BF16), wbr_ref[1])
         + _sigmoid(ug[:, 2 * width:3 * width]) * _dot(yc_ref[0].astype(BF16), wbr_ref[2]))
    out = _dot(m.astype(BF16), wo_ref[...])
    o_ref[0] = x_ref[0] + mod_ref[0, 0, 2:3, :] * _rms(out, g_ref[...])


def _merge(xs, u_g, u_b, wkv_out, lru_out, y_c, mod, p, ng1, lc, skip_ctx):
    b, r, d = xs.shape
    width = y_c.shape[2]
    tm = ROW_TILE
    off = lc // tm if skip_ctx else 0
    nt = r // tm - off
    nctx = lc // tm
    heads = width // HEAD_DIM
    hid = np.arange(width) // HEAD_DIM
    e = (hid[:, None] == np.arange(V7X_LANES)[None, :]).astype(np.float32)
    e_bf, et_bf = jnp.asarray(e, BF16), jnp.asarray(e.T, BF16)
    assert heads <= V7X_LANES
    rowblk = lambda i, t: (i, t + off, 0)
    const2 = lambda i, t: (0, 0)
    act = lambda wd: pl.BlockSpec((1, tm, wd), rowblk)
    return pl.pallas_call(
        _merge_kernel,
        grid=(b, nt),
        in_specs=[
            act(d), act(3 * width), act(width), act(width), act(width), act(width), act(width), act(width),
            pl.BlockSpec((1, tm, width), lambda i, t: (i, t + off, 1)),
            act(width),
            pl.BlockSpec((1, 1, 6, d), lambda i, t: (i, jnp.where(t + off < nctx, 0, 1), 0, 0)),
            pl.BlockSpec((1, width), const2),
            pl.BlockSpec((1, width), const2),
            pl.BlockSpec((width, V7X_LANES), const2),
            pl.BlockSpec((V7X_LANES, width), const2),
            pl.BlockSpec((3, width, d), lambda i, t: (0, 0, 0)),
            pl.BlockSpec((d, d), const2),
            pl.BlockSpec((1, d), const2),
        ],
        out_specs=pl.BlockSpec((1, tm, d), lambda i, t: (i, t, 0)),
        out_shape=jax.ShapeDtypeStruct((b, nt * tm, d), F32),
        compiler_params=_cparams(("parallel", "parallel")),
        name="merge",
    )(xs, u_g, wkv_out[0], wkv_out[1], wkv_out[2], wkv_out[3], lru_out[0], lru_out[1], u_b, y_c,
      mod, p["rwkv_lnx_w"].reshape(1, -1), p["rwkv_lnx_b"].reshape(1, -1), e_bf, et_bf,
      p["w_branch"].astype(BF16), p["w_out"].astype(BF16), ng1)


def _ffn_kernel(x_ref, xp_ref, xn_ref, mod_ref, g2_ref, g3_ref, wi_ref, cw_ref, cb_ref, wo_ref, o_ref,
                *, tm, nctx, nt, dff):
    t = pl.program_id(1)
    first = jnp.logical_or(t == 0, t == nctx)
    last = jnp.logical_or(t == nctx - 1, t == nt - 1)
    sh, sc, gate_mod = mod_ref[0, 0, 3:4, :], mod_ref[0, 0, 4:5, :], mod_ref[0, 0, 5:6, :]

    def hidden(xv):
        return (_rms(xv, g2_ref[...]) * (1.0 + sc) + sh).astype(BF16)

    x = x_ref[0]
    h = hidden(x)
    gate = _dot(h, wi_ref[:, 0:dff])
    up = _dot(h, wi_ref[:, dff:2 * dff])
    g_prev = _dot(hidden(xp_ref[0]), wi_ref[:, 0:dff])[FFN_HALO - 1:FFN_HALO, :]
    g_next = _dot(hidden(xn_ref[0]), wi_ref[:, 0:dff])[0:1, :]
    g_prev = jnp.where(first, 0.0, g_prev)
    g_next = jnp.where(last, 0.0, g_next)
    rows = lax.broadcasted_iota(jnp.int32, (tm, 1), 0)
    gate_m1 = jnp.where(rows == 0, g_prev, pltpu.roll(gate, 1, 0))
    gate_p1 = jnp.where(rows == tm - 1, g_next, pltpu.roll(gate, tm - 1, 0))
    gc = cw_ref[0:1, :] * gate_m1 + cw_ref[1:2, :] * gate + cw_ref[2:3, :] * gate_p1 + cb_ref[...]
    act = (gc * _sigmoid(gc) * up).astype(BF16)
    out = _dot(act, wo_ref[...])
    o_ref[0] = x + gate_mod * _rms(out, g3_ref[...])


def _ffn(x1, mod, p, ng2, ng3, lc):
    b, r, d = x1.shape
    tm = ROW_TILE
    nt = r // tm
    nctx = lc // tm
    hb = tm // FFN_HALO
    nhalo = r // FFN_HALO
    dff = p["ffn_w_out"].shape[0]
    const2 = lambda i, t: (0, 0)
    return pl.pallas_call(
        functools.partial(_ffn_kernel, tm=tm, nctx=nctx, nt=nt, dff=dff),
        grid=(b, nt),
        in_specs=[
            pl.BlockSpec((1, tm, d), lambda i, t: (i, t, 0)),
            pl.BlockSpec((1, FFN_HALO, d), lambda i, t: (i, jnp.maximum(t * hb - 1, 0), 0)),
            pl.BlockSpec((1, FFN_HALO, d), lambda i, t: (i, jnp.minimum((t + 1) * hb, nhalo - 1), 0)),
            pl.BlockSpec((1, 1, 6, d), lambda i, t: (i, jnp.where(t < nctx, 0, 1), 0, 0)),
            pl.BlockSpec((1, d), const2),
            pl.BlockSpec((1, d), const2),
            pl.BlockSpec((d, 2 * dff), const2),
            pl.BlockSpec((3, dff), const2),
            pl.BlockSpec((1, dff), const2),
            pl.BlockSpec((dff, d), const2),
        ],
        out_specs=pl.BlockSpec((1, tm, d), lambda i, t: (i, t, 0)),
        out_shape=jax.ShapeDtypeStruct((b, r, d), F32),
        compiler_params=_cparams(("parallel", "parallel")),
        name="conv_ffn",
    )(x1, x1, x1, mod, ng2, ng3, p["ffn_w_in"].astype(BF16), p["ffn_conv_w"],
      p["ffn_conv_b"].reshape(1, -1), p["ffn_w_out"].astype(BF16))


def _in_tile(r):
    for parts in (4, 8, 2, 1, 16, 32):
        if r % parts == 0 and (r // parts) % V7X_SUBLANES == 0 and r // parts <= 1152:
            return r // parts
    raise ValueError(f"no row tile for {r} rows")


def _col_tile(n):
    for parts in (1, 2, 3, 4, 6, 8):
        if n % parts == 0 and (n // parts) % V7X_LANES == 0 and n // parts <= 1792:
            return n // parts
    raise ValueError(f"no column tile for {n} columns")


def _layer(xs, mod, p, lc, update_ctx):
    width = p["rwkv_k_k"].shape[0]
    a_cols = 3 * width + W_LORA + A_LORA + G_LORA
    bounds = np.cumsum([0, a_cols, 2 * width, 3 * width, 3 * width])
    ng = p["norm_g"]
    w_in = p["w_in"].astype(BF16)
    tm = _in_tile(xs.shape[1])
    u = [_in_proj(xs, mod, ng[0:1], w_in[:, lo:hi], lc, tm, _col_tile(int(hi - lo)))
         for lo, hi in zip(bounds[:-1], bounds[1:])]
    u_a, u_b, u_c, u_g = u
    wkv_out = _wkv(u_a, p, lc)
    lru_out = _lru(u_b, p, lc)
    y_c = _na(u_c, p["na_rpb"], lc, update_ctx)
    x1 = _merge(xs, u_g, u_b, wkv_out, lru_out, y_c, mod, p, ng[1:2], lc, not update_ctx)
    return _ffn(x1, mod, p, ng[2:3], ng[3:4], lc if update_ctx else 0)


_PARAM_NAMES = ("norm_g", "w_in", "rwkv_mu", "rwkv_w0", "rwkv_w_up", "rwkv_a0", "rwkv_a_up", "rwkv_g_up",
                "rwkv_k_k", "rwkv_k_a", "rwkv_r_k", "rwkv_lnx_w", "rwkv_lnx_b", "lru_conv_w", "lru_conv_b",
                "lru_gate_a_w", "lru_gate_a_b", "lru_gate_x_w", "lru_gate_x_b", "lru_lambda", "na_rpb",
                "w_branch", "w_out", "ffn_w_in", "ffn_conv_w", "ffn_conv_b", "ffn_w_out")


def kernel(x, c, ctx, c_ctx, ada_w, ada_b, norm_g, w_in, rwkv_mu, rwkv_w0, rwkv_w_up, rwkv_a0, rwkv_a_up, rwkv_g_up, rwkv_k_k, rwkv_k_a, rwkv_r_k, rwkv_lnx_w, rwkv_lnx_b, lru_conv_w, lru_conv_b, lru_gate_a_w, lru_gate_a_b, lru_gate_x_w, lru_gate_x_b, lru_lambda, na_rpb, w_branch, w_out, ffn_w_in, ffn_conv_w, ffn_conv_b, ffn_w_out):
    stacked = dict(zip(_PARAM_NAMES, (norm_g, w_in, rwkv_mu, rwkv_w0, rwkv_w_up, rwkv_a0, rwkv_a_up,
                                      rwkv_g_up, rwkv_k_k, rwkv_k_a, rwkv_r_k, rwkv_lnx_w, rwkv_lnx_b,
                                      lru_conv_w, lru_conv_b, lru_gate_a_w, lru_gate_a_b, lru_gate_x_w,
                                      lru_gate_x_b, lru_lambda, na_rpb, w_branch, w_out, ffn_w_in,
                                      ffn_conv_w, ffn_conv_b, ffn_w_out)))
    depth = ada_w.shape[0]
    b, _, d = x.shape
    lc = ctx.shape[1]
    pad = (-(b + 1)) % V7X_SUBLANES
    cc = jnp.concatenate([c, c_ctx[None, :], jnp.zeros((pad, d), c.dtype)], axis=0)
    mod_all = _modulation(cc, ada_w, ada_b)
    xs = jnp.concatenate([ctx, x], axis=1)
    for l in range(depth):
        p = {name: val[l] for name, val in stacked.items()}
        p["rwkv_r_k"] = p["rwkv_r_k"].reshape(-1)
        m = mod_all[l].reshape(-1, 6, d)
        mod = jnp.stack([jnp.broadcast_to(m[b], (b, 6, d)), m[:b]], axis=1)
        xs = _layer(xs, mod, p, lc, l < depth - 1)
    return xs
```

```python
import functools

import numpy as np
import jax
import jax.numpy as jnp
from jax import lax
from jax.experimental import pallas as pl
from jax.experimental.pallas import tpu as pltpu

F32 = jnp.float32
BF16 = jnp.bfloat16

GRID_W = 64
HEAD_DIM = 64
W_LORA = 64
A_LORA = 64
G_LORA = 128
LNX_EPS = 64e-5
LRU_C = 8.0
NA_KH = 8
NA_KW = 16
NA_QROWS = 2
NA_BLOCKS_PER_ITER = 2
NA_BAND_ROWS = 10
ROPE_BASE = 10000.0
EPS = 1e-6
NEG_INF = -1e30

V7X_LANES = 128
V7X_SUBLANES = 8
V7X_VMEM_LIMIT_BYTES = 56 * 1024 * 1024

PAIR_W = 2 * HEAD_DIM
WKV_CHUNK = 64
WKV_PAIRS_PER_ITER = 4
LRU_CHUNK = 128
ROW_TILE = 256
LRU_GROUP = 256
FFN_HALO = 16


def _cparams(sem):
    return pltpu.CompilerParams(dimension_semantics=sem, vmem_limit_bytes=V7X_VMEM_LIMIT_BYTES)


def _split_bf16(x, parts):
    out = []
    rem = x
    for _ in range(parts - 1):
        hi = rem.astype(BF16)
        out.append(hi)
        rem = rem - hi.astype(F32)
    out.append(rem.astype(BF16))
    return out


def _dot(a, b):
    return jnp.dot(a, b, preferred_element_type=F32)


def _dot_nt(a, b):
    return lax.dot_general(a, b, (((1,), (1,)), ((), ())), preferred_element_type=F32)


def _dot_tn(a, b):
    return lax.dot_general(a, b, (((0,), (0,)), ((), ())), preferred_element_type=F32)


def _dot_exact_lhs(a_bf16, x, parts):
    acc = None
    for piece in _split_bf16(x, parts):
        t = _dot(a_bf16, piece)
        acc = t if acc is None else acc + t
    return acc


def _dot_exact_rhs(x, b_bf16, parts):
    acc = None
    for piece in _split_bf16(x, parts):
        t = _dot(piece, b_bf16)
        acc = t if acc is None else acc + t
    return acc


def _sigmoid(x):
    return 1.0 / (1.0 + jnp.exp(-x))


def _softplus(x):
    return jnp.maximum(x, 0.0) + jnp.log(1.0 + jnp.exp(-jnp.abs(x)))


def _rms(x, g):
    return x * lax.rsqrt(jnp.mean(x * x, axis=-1, keepdims=True) + EPS) * g


def _mod_kernel(c_ref, w_ref, b_ref, o_ref):
    c = c_ref[...]
    s = c * _sigmoid(c)
    o_ref[0] = _dot(s.astype(BF16), w_ref[0].astype(BF16)) + b_ref[0]


def _modulation(cc, ada_w, ada_b):
    depth, d, n = ada_w.shape
    rows = cc.shape[0]
    tn = n // 4
    return pl.pallas_call(
        _mod_kernel,
        grid=(depth, n // tn),
        in_specs=[
            pl.BlockSpec((rows, d), lambda l, j: (0, 0)),
            pl.BlockSpec((1, d, tn), lambda l, j: (l, 0, j)),
            pl.BlockSpec((1, 1, tn), lambda l, j: (l, 0, j)),
        ],
        out_specs=pl.BlockSpec((1, rows, tn), lambda l, j: (l, 0, j)),
        out_shape=jax.ShapeDtypeStruct((depth, rows, n), F32),
        compiler_params=_cparams(("parallel", "parallel")),
        name="modulation",
    )(cc, ada_w, ada_b.reshape(depth, 1, n))


def _in_kernel(x_ref, mod_ref, g_ref, w_ref, o_ref, hb_ref, *, lc, tm):
    @pl.when(pl.program_id(2) == 0)
    def _():
        y = _rms(x_ref[0], g_ref[...])
        row = pl.program_id(1) * tm + lax.broadcasted_iota(jnp.int32, (tm, 1), 0)
        is_ctx = row < lc
        sh = jnp.where(is_ctx, mod_ref[0, 0, 0:1, :], mod_ref[0, 1, 0:1, :])
        sc = jnp.where(is_ctx, mod_ref[0, 0, 1:2, :], mod_ref[0, 1, 1:2, :])
        hb_ref[...] = (y * (1.0 + sc) + sh).astype(BF16)

    o_ref[0] = _dot(hb_ref[...], w_ref[...])


def _in_proj(xs, mod, g, w, lc, tm, tn):
    b, r, d = xs.shape
    n = w.shape[1]
    return pl.pallas_call(
        functools.partial(_in_kernel, lc=lc, tm=tm),
        grid=(b, r // tm, n // tn),
        in_specs=[
            pl.BlockSpec((1, tm, d), lambda i, t, j: (i, t, 0)),
            pl.BlockSpec((1, 2, 6, d), lambda i, t, j: (i, 0, 0, 0)),
            pl.BlockSpec((1, d), lambda i, t, j: (0, 0)),
            pl.BlockSpec((d, tn), lambda i, t, j: (0, j)),
        ],
        out_specs=pl.BlockSpec((1, tm, tn), lambda i, t, j: (i, t, j)),
        out_shape=jax.ShapeDtypeStruct((b, r, n), F32),
        scratch_shapes=[pltpu.VMEM((tm, d), BF16)],
        compiler_params=_cparams(("parallel", "parallel", "arbitrary")),
        name="in_proj",
    )(xs, mod, g, w)


def _stack_pair(x, lane_a):
    return jnp.concatenate([jnp.where(lane_a, x, 0.0), jnp.where(lane_a, 0.0, x)], axis=0)


def _wkv_chains(chains, bd, c):
    n = 2 * c
    lane_a = lax.broadcasted_iota(jnp.int32, (c, PAIR_W), 1) < HEAD_DIM
    stack = lambda x: _stack_pair(x, lane_a).astype(BF16)
    nch = len(chains)

    sq = [_split_bf16(ch[3] * ch[3], 2) for ch in chains]
    ss_hi = [_dot(s[0], bd) for s in sq]
    ss_lo = [_dot(s[1], bd) for s in sq]

    ops = []
    for ch, hi, lo in zip(chains, ss_hi, ss_lo):
        r, kd, v, kkraw, a, lw, cum = ch[:7]
        kkn = kkraw * lax.rsqrt(jnp.maximum(hi + lo, 1e-24))
        tot = jnp.sum(lw, axis=0, keepdims=True)
        e_inv = jnp.exp(-cum)
        e_rem = jnp.exp(tot - cum)
        akk = a * kkn
        lhs = jnp.concatenate([stack(-kkn * jnp.exp(cum - lw)), stack(r * jnp.exp(cum))], axis=0)
        bk = jnp.concatenate([stack(akk * e_inv), stack(kd * e_inv)], axis=0)
        bkc = jnp.concatenate([stack(akk * e_rem), stack(kd * e_rem)], axis=0)
        ops.append((lhs, bk, bkc, stack(v), tot))

    g = [_dot_nt(o[0], o[1]) for o in ops]
    ls = [_dot_nt(o[0], ch[7].astype(BF16)) for o, ch in zip(ops, chains)]
    m_b = [jnp.where(ch[8], gi[:n, :n], 0.0) for ch, gi in zip(chains, g)]
    m_k = [jnp.where(ch[8], gi[:n, n:], 0.0).astype(BF16) for ch, gi in zip(chains, g)]
    n_bk = [jnp.concatenate([jnp.where(ch[9], gi[n:, :n], 0.0).astype(BF16),
                             jnp.where(ch[9], gi[n:, n:], 0.0).astype(BF16)], axis=1)
            for ch, gi in zip(chains, g)]
    mkv = [_dot(mk, o[3]) for mk, o in zip(m_k, ops)]

    eye = (lax.broadcasted_iota(jnp.int32, (n, n), 0) == lax.broadcasted_iota(jnp.int32, (n, n), 1))
    t_inv = [jnp.where(eye, 1.0, 0.0) + m for m in m_b]
    m_pow = [m.astype(BF16) for m in m_b]
    span = 1
    while 2 * span < c:
        m_pow = [_dot(mp, mp).astype(BF16) for mp in m_pow]
        t_inv = [t + _dot(t.astype(BF16), mp) for t, mp in zip(t_inv, m_pow)]
        span *= 2

    u = [_dot(t.astype(BF16), (l[:n] + mv).astype(BF16)).astype(BF16)
         for t, l, mv in zip(t_inv, ls, mkv)]
    uv = [jnp.concatenate([ui, o[3]], axis=0) for ui, o in zip(u, ops)]
    ys = [l[n:] + _dot(nb, x) for l, nb, x in zip(ls, n_bk, uv)]
    s_new = [ch[7] * jnp.exp(o[4]) + _dot_tn(x, o[2]) for ch, o, x in zip(chains, ops, uv)]
    return [(ys[j][:c] + ys[j][c:], s_new[j]) for j in range(nch)]


def _wkv_kernel(uf_ref, ufp_ref, ufn_ref, ub_ref, ubp_ref, ubn_ref,
                mu_ref, w0_ref, wup_ref, a0_ref, aup_ref, gup_ref, kk_ref, ka_ref, rk_ref,
                tri_ref, ms_ref, mi_ref, bd_ref,
                yf_ref, yb_ref, eb_ref, eg_ref,
                s_ref, stg_ref, rkk_ref, yst_ref, bst_ref, *, c, nc, nl):
    i = pl.program_id(1)
    width = yf_ref.shape[2]
    npair = width // PAIR_W
    cb = jnp.where(i < nc, nc - 1 - i, 2 * nc + nl - 1 - i)

    @pl.when(i == 0)
    def _():
        s_ref[...] = jnp.zeros(s_ref.shape, F32)

    rows = lax.broadcasted_iota(jnp.int32, (c, 1), 0)

    def shifted(u_ref, p_ref, n_ref, ci):
        u = u_ref[0]
        first = jnp.logical_or(ci == 0, ci == nc)
        last = jnp.logical_or(ci == nc - 1, ci == nc + nl - 1)
        prev_row = jnp.where(first, 0.0, p_ref[0, V7X_SUBLANES - 1:V7X_SUBLANES, :])
        next_row = jnp.where(last, 0.0, n_ref[0, 0:1, :])
        u_prev = jnp.where(rows == 0, prev_row, pltpu.roll(u, 1, 0))
        u_next = jnp.where(rows == c - 1, next_row, pltpu.roll(u, c - 1, 0))
        return u + mu_ref[0:1, :] * (u_prev - u) + mu_ref[1:2, :] * (u_next - u)

    def lora_a(a_lo, d):
        return _sigmoid(a0_ref[d:d + 1, :] + _dot(a_lo, aup_ref[d]))

    def prepare(us, d):
        r = us[:, 0:width]
        k = us[:, width:2 * width]
        v = us[:, 2 * width:3 * width]
        o = 3 * width
        w_lo = jnp.tanh(us[:, o:o + W_LORA]).astype(BF16)
        a_lo = us[:, o + W_LORA:o + W_LORA + A_LORA].astype(BF16)
        wpre = w0_ref[d:d + 1, :] + _dot(w_lo, wup_ref[d])
        lw = -jnp.exp(-_softplus(-wpre) - 0.5)
        a = lora_a(a_lo, d)
        kd = k * (1.0 + (a - 1.0) * ka_ref[...])
        kkraw = k * kk_ref[...]
        cum = _dot_exact_lhs(tri_ref[d], lw, 3)
        for q, arr in enumerate((r, kd, v, kkraw, a, lw, cum)):
            for p in range(npair):
                stg_ref[d, q, p] = arr[:, p * PAIR_W:(p + 1) * PAIR_W]
        return r, k, kd, a_lo

    us_f = shifted(uf_ref, ufp_ref, ufn_ref, i)
    r_f, k_f, kd_f, a_lo_f = prepare(us_f, 0)
    a_rev = lora_a(a_lo_f, 1)
    kd_rev = k_f * (1.0 + (a_rev - 1.0) * ka_ref[...])
    rkk = r_f * (kd_f + kd_rev) * rk_ref[...]
    for p in range(npair):
        rkk_ref[p] = rkk[:, p * PAIR_W:(p + 1) * PAIR_W]
    o = 3 * width + W_LORA + A_LORA
    g_lo = _sigmoid(us_f[:, o:o + G_LORA]).astype(BF16)
    eg_ref[0] = _dot(g_lo, gup_ref[...])

    us_b = shifted(ub_ref, ubp_ref, ubn_ref, cb)
    prepare(us_b, 1)

    bd = bd_ref[...]

    def pair_body(it, carry):
        pairs = [it * WKV_PAIRS_PER_ITER + j for j in range(WKV_PAIRS_PER_ITER)]
        keys = [(d, p) for p in pairs for d in range(2)]
        masks = [(ms_ref[d] > 0.5, mi_ref[d] > 0.5) for d in range(2)]
        chains = [tuple(stg_ref[d, q, p] for q in range(7)) + (s_ref[d, p],) + masks[d]
                  for d, p in keys]
        rk_split = [_split_bf16(rkk_ref[p], 2) for p in pairs]
        rk_hi = [_dot(s[0], bd) for s in rk_split]
        rk_lo = [_dot(s[1], bd) for s in rk_split]
        for (d, p), (y, s_new) in zip(keys, _wkv_chains(chains, bd, c)):
            s_ref[d, p] = s_new
            yst_ref[d, p] = y
        for p, hi, lo in zip(pairs, rk_hi, rk_lo):
            bst_ref[p] = (hi + lo) * stg_ref[0, 2, p]
        return carry

    lax.fori_loop(0, npair // WKV_PAIRS_PER_ITER, pair_body, 0)

    for p in range(npair):
        sl = slice(p * PAIR_W, (p + 1) * PAIR_W)
        yf_ref[0, :, sl] = yst_ref[0, p]
        yb_ref[0, :, sl] = yst_ref[1, p]
        eb_ref[0, :, sl] = bst_ref[p]


def _wkv_consts(c):
    t = np.arange(c)
    lower = (t[:, None] >= t[None, :])
    tri = np.stack([lower, lower.T]).astype(np.float32)
    strict = np.stack([t[:, None] > t[None, :], t[:, None] < t[None, :]])
    incl = np.stack([lower, lower.T])
    ms = np.tile(strict, (1, 2, 2)).astype(np.float32)
    mi = np.tile(incl, (1, 2, 2)).astype(np.float32)
    h = np.arange(PAIR_W) // HEAD_DIM
    bd = (h[:, None] == h[None, :]).astype(np.float32)
    return jnp.asarray(tri, BF16), jnp.asarray(ms), jnp.asarray(mi), jnp.asarray(bd, BF16)


def _wkv(u_a, p, lc):
    b, r, fa = u_a.shape
    width = p["rwkv_k_k"].shape[0]
    c = WKV_CHUNK
    nc, nl = lc // c, (r - lc) // c
    nchunk = nc + nl
    hb = c // V7X_SUBLANES
    nhalo = r // V7X_SUBLANES
    npair = width // PAIR_W
    tri, ms, mi, bd = _wkv_consts(c)

    def rev(i):
        return jnp.where(i < nc, nc - 1 - i, 2 * nc + nl - 1 - i)

    cur_f = lambda bi, i: (bi, i, 0)
    prev_f = lambda bi, i: (bi, jnp.maximum(i * hb - 1, 0), 0)
    next_f = lambda bi, i: (bi, jnp.minimum((i + 1) * hb, nhalo - 1), 0)
    cur_b = lambda bi, i: (bi, rev(i), 0)
    prev_b = lambda bi, i: (bi, jnp.maximum(rev(i) * hb - 1, 0), 0)
    next_b = lambda bi, i: (bi, jnp.minimum((rev(i) + 1) * hb, nhalo - 1), 0)
    const2 = lambda bi, i: (0, 0)
    const3 = lambda bi, i: (0, 0, 0)

    row = lambda x: x.reshape(1, -1)
    out_sd = jax.ShapeDtypeStruct((b, r, width), F32)
    return pl.pallas_call(
        functools.partial(_wkv_kernel, c=c, nc=nc, nl=nl),
        grid=(b, nchunk),
        in_specs=[
            pl.BlockSpec((1, c, fa), cur_f),
            pl.BlockSpec((1, V7X_SUBLANES, fa), prev_f),
            pl.BlockSpec((1, V7X_SUBLANES, fa), next_f),
            pl.BlockSpec((1, c, fa), cur_b),
            pl.BlockSpec((1, V7X_SUBLANES, fa), prev_b),
            pl.BlockSpec((1, V7X_SUBLANES, fa), next_b),
            pl.BlockSpec((2, fa), const2),
            pl.BlockSpec((2, width), const2),
            pl.BlockSpec((2, W_LORA, width), const3),
            pl.BlockSpec((2, width), const2),
            pl.BlockSpec((2, A_LORA, width), const3),
            pl.BlockSpec((G_LORA, width), const2),
            pl.BlockSpec((1, width), const2),
            pl.BlockSpec((1, width), const2),
            pl.BlockSpec((1, width), const2),
            pl.BlockSpec((2, c, c), const3),
            pl.BlockSpec((2, 2 * c, 2 * c), const3),
            pl.BlockSpec((2, 2 * c, 2 * c), const3),
            pl.BlockSpec((PAIR_W, PAIR_W), const2),
        ],
        out_specs=[
            pl.BlockSpec((1, c, width), cur_f),
            pl.BlockSpec((1, c, width), cur_b),
            pl.BlockSpec((1, c, width), cur_f),
            pl.BlockSpec((1, c, width), cur_f),
        ],
        out_shape=[out_sd, out_sd, out_sd, out_sd],
        scratch_shapes=[
            pltpu.VMEM((2, npair, PAIR_W, PAIR_W), F32),
            pltpu.VMEM((2, 7, npair, c, PAIR_W), F32),
            pltpu.VMEM((npair, c, PAIR_W), F32),
            pltpu.VMEM((2, npair, c, PAIR_W), F32),
            pltpu.VMEM((npair, c, PAIR_W), F32),
        ],
        compiler_params=_cparams(("parallel", "arbitrary")),
        name="wkv7",
    )(u_a, u_a, u_a, u_a, u_a, u_a,
      p["rwkv_mu"], p["rwkv_w0"], p["rwkv_w_up"].astype(BF16), p["rwkv_a0"],
      p["rwkv_a_up"].astype(BF16), p["rwkv_g_up"].astype(BF16),
      row(p["rwkv_k_k"]), row(p["rwkv_k_a"]), row(p["rwkv_r_k"]),
      tri, ms, mi, bd)


def _lru_scan(a, bv, h0, c, reverse):
    rows = lax.broadcasted_iota(jnp.int32, (c, 1), 0)
    s = 1
    while s < c:
        if reverse:
            keep = rows < c - s
            a_sh = jnp.where(keep, pltpu.roll(a, c - s, 0), 1.0)
            b_sh = jnp.where(keep, pltpu.roll(bv, c - s, 0), 0.0)
        else:
            keep = rows >= s
            a_sh = jnp.where(keep, pltpu.roll(a, s, 0), 1.0)
            b_sh = jnp.where(keep, pltpu.roll(bv, s, 0), 0.0)
        bv = a * b_sh + bv
        a = a * a_sh
        s *= 2
    return bv + a * h0


def _lru_kernel(uf_ref, ufp_ref, ufn_ref, ub_ref, ubp_ref, ubn_ref,
                cw_ref, cbias_ref, wa_ref, ba_ref, wx_ref, bx_ref, lam_ref,
                hf_ref, hb_ref, h_ref, *, c, nc, nl):
    i = pl.program_id(1)
    width = hf_ref.shape[2]
    cb = jnp.where(i < nc, nc - 1 - i, 2 * nc + nl - 1 - i)

    @pl.when(i == 0)
    def _():
        h_ref[...] = jnp.zeros(h_ref.shape, F32)

    rows = lax.broadcasted_iota(jnp.int32, (c, 1), 0)

    def coeffs(u_ref, p_ref, n_ref, ci, d):
        xb = u_ref[0]
        first = jnp.logical_or(ci == 0, ci == nc)
        last = jnp.logical_or(ci == nc - 1, ci == nc + nl - 1)
        p6 = jnp.where(first, 0.0, p_ref[0, V7X_SUBLANES - 2:V7X_SUBLANES - 1, :])
        p7 = jnp.where(first, 0.0, p_ref[0, V7X_SUBLANES - 1:V7X_SUBLANES, :])
        n0 = jnp.where(last, 0.0, n_ref[0, 0:1, :])
        x_m2 = jnp.where(rows == 0, p6, jnp.where(rows == 1, p7, pltpu.roll(xb, 2, 0)))
        x_m1 = jnp.where(rows == 0, p7, pltpu.roll(xb, 1, 0))
        x_p1 = jnp.where(rows == c - 1, n0, pltpu.roll(xb, c - 1, 0))
        xc = (cw_ref[0:1, :] * x_m2 + cw_ref[1:2, :] * x_m1 + cw_ref[2:3, :] * xb
              + cw_ref[3:4, :] * x_p1 + cbias_ref[...])
        xcb = xc.astype(BF16)
        ng = width // LRU_GROUP
        ga = jnp.concatenate([_dot(xcb[:, g * LRU_GROUP:(g + 1) * LRU_GROUP], wa_ref[d, g])
                              for g in range(ng)], axis=1)
        gx = jnp.concatenate([_dot(xcb[:, g * LRU_GROUP:(g + 1) * LRU_GROUP], wx_ref[d, g])
                              for g in range(ng)], axis=1)
        rg = _sigmoid(ga + ba_ref[d:d + 1, :])
        ig = _sigmoid(gx + bx_ref[d:d + 1, :])
        a = jnp.exp(-LRU_C * rg * _softplus(-lam_ref[d:d + 1, :]))
        bv = jnp.sqrt(1.0 - a * a) * (ig * xc)
        return a, bv

    a_f, b_f = coeffs(uf_ref, ufp_ref, ufn_ref, i, 0)
    h_f = _lru_scan(a_f, b_f, h_ref[0:1, :], c, False)
    h_ref[0:1, :] = h_f[c - 1:c, :]
    hf_ref[0] = h_f

    a_b, b_b = coeffs(ub_ref, ubp_ref, ubn_ref, cb, 1)
    h_b = _lru_scan(a_b, b_b, h_ref[1:2, :], c, True)
    h_ref[1:2, :] = h_b[0:1, :]
    hb_ref[0] = h_b


def _block_diag_groups(w):
    nd, nb, bw, _ = w.shape
    per = LRU_GROUP // bw
    w = w.reshape(nd, nb // per, per, bw, bw)
    eye = jnp.eye(per, dtype=w.dtype)
    out = jnp.einsum("dgpij,pq->dgpiqj", w, eye)
    return out.reshape(nd, nb // per, LRU_GROUP, LRU_GROUP).astype(BF16)


def _lru(u_b, p, lc):
    b, r, fb = u_b.shape
    width = fb // 2
    c = LRU_CHUNK
    nc, nl = lc // c, (r - lc) // c
    nchunk = nc + nl
    hb = c // V7X_SUBLANES
    nhalo = r // V7X_SUBLANES
    ng = width // LRU_GROUP

    def rev(i):
        return jnp.where(i < nc, nc - 1 - i, 2 * nc + nl - 1 - i)

    cur_f = lambda bi, i: (bi, i, 0)
    prev_f = lambda bi, i: (bi, jnp.maximum(i * hb - 1, 0), 0)
    next_f = lambda bi, i: (bi, jnp.minimum((i + 1) * hb, nhalo - 1), 0)
    cur_b = lambda bi, i: (bi, rev(i), 0)
    prev_b = lambda bi, i: (bi, jnp.maximum(rev(i) * hb - 1, 0), 0)
    next_b = lambda bi, i: (bi, jnp.minimum((rev(i) + 1) * hb, nhalo - 1), 0)
    const2 = lambda bi, i: (0, 0)
    const4 = lambda bi, i: (0, 0, 0, 0)
    out_sd = jax.ShapeDtypeStruct((b, r, width), F32)
    return pl.pallas_call(
        functools.partial(_lru_kernel, c=c, nc=nc, nl=nl),
        grid=(b, nchunk),
        in_specs=[
            pl.BlockSpec((1, c, width), cur_f),
            pl.BlockSpec((1, V7X_SUBLANES, width), prev_f),
            pl.BlockSpec((1, V7X_SUBLANES, width), next_f),
            pl.BlockSpec((1, c, width), cur_b),
            pl.BlockSpec((1, V7X_SUBLANES, width), prev_b),
            pl.BlockSpec((1, V7X_SUBLANES, width), next_b),
            pl.BlockSpec((4, width), const2),
            pl.BlockSpec((1, width), const2),
            pl.BlockSpec((2, ng, LRU_GROUP, LRU_GROUP), const4),
            pl.BlockSpec((2, width), const2),
            pl.BlockSpec((2, ng, LRU_GROUP, LRU_GROUP), const4),
            pl.BlockSpec((2, width), const2),
            pl.BlockSpec((2, width), const2),
        ],
        out_specs=[pl.BlockSpec((1, c, width), cur_f), pl.BlockSpec((1, c, width), cur_b)],
        out_shape=[out_sd, out_sd],
        scratch_shapes=[pltpu.VMEM((2, width), F32)],
        compiler_params=_cparams(("parallel", "arbitrary")),
        name="rglru",
    )(u_b, u_b, u_b, u_b, u_b, u_b,
      p["lru_conv_w"], p["lru_conv_b"].reshape(1, -1),
      _block_diag_groups(p["lru_gate_a_w"]), p["lru_gate_a_b"],
      _block_diag_groups(p["lru_gate_x_w"]), p["lru_gate_x_b"], p["lru_lambda"])


def _na_geometry(rows):
    assert rows >= NA_BAND_ROWS
    kh = min(NA_KH, rows)
    kbh = NA_BAND_ROWS
    deltas = []
    for blk in range(rows // NA_QROWS):
        r0 = blk * NA_QROWS
        kr = int(np.clip(r0 - kh // 2, 0, rows - kbh))
        deltas.append(r0 - kr)
    return kh, kbh, deltas, sorted(set(deltas))


def _na_bias_table(rpb, rows):
    w = GRID_W
    kh, kbh, deltas, geos = _na_geometry(rows)
    nq, nk = NA_QROWS * w, kbh * w
    drow = np.zeros((len(geos), nq, nk), np.int32)
    dcol = np.zeros((len(geos), nq, nk), np.int32)
    ok = np.zeros((len(geos), nq, nk), bool)
    qc = np.tile(np.arange(w), NA_QROWS)
    qr = np.repeat(np.arange(NA_QROWS), w)
    kc = np.tile(np.arange(w), kbh)
    kj = np.repeat(np.arange(kbh), w)
    cs = np.clip(qc - NA_KW // 2, 0, w - NA_KW)
    col_ok = (kc[None, :] >= cs[:, None]) & (kc[None, :] < cs[:, None] + NA_KW)
    dc = np.clip(kc[None, :] - qc[:, None] + NA_KW - 1, 0, 2 * NA_KW - 2)
    for gi, delta in enumerate(geos):
        seen = None
        for blk, dl in enumerate(deltas):
            if dl != delta:
                continue
            r0 = blk * NA_QROWS
            kr = r0 - delta
            qrow = r0 + qr
            krow = kr + kj
            rs = np.clip(qrow - kh // 2, 0, rows - kh)
            row_ok = (krow[None, :] >= rs[:, None]) & (krow[None, :] < rs[:, None] + kh)
            dr = np.clip(krow[None, :] - qrow[:, None] + NA_KH - 1, 0, 2 * NA_KH - 2)
            cur = (row_ok & col_ok, dr)
            if seen is None:
                seen = cur
            else:
                assert np.array_equal(seen[0], cur[0]) and np.array_equal(seen[1], cur[1])
        ok[gi], drow[gi], dcol[gi] = seen[0], seen[1], dc
    dr5 = drow.reshape(len(geos), NA_QROWS, w, kbh, w)[:, :, 0, :, 0]
    dc4 = dc.reshape(NA_QROWS, w, kbh, w)[0, :, 0, :]
    oh_r = (dr5[..., None] == np.arange(rpb.shape[1])).astype(np.float32)
    oh_c = (dc4[None] == np.arange(rpb.shape[2])[:, None, None]).astype(np.float32)
    rows_sel = jnp.einsum("gqjr,hrc->hgqjc", oh_r, rpb, precision=lax.Precision.HIGHEST)
    bias = jnp.einsum("hgqjc,cxy->hgqxjy", rows_sel, oh_c, precision=lax.Precision.HIGHEST)
    bias = bias.reshape(rpb.shape[0], len(geos), nq, nk)
    bias = jnp.where(ok[None], bias, NEG_INF)
    h = rpb.shape[0]
    bias = bias.reshape(h // 2, 2, len(geos), nq, nk).transpose(0, 2, 1, 3, 4)
    return bias, deltas, geos


def _rope_tables(t):
    half = HEAD_DIM // 2
    freqs = ROPE_BASE ** (-jnp.arange(0, half, 2, dtype=F32) / half)
    pos = jnp.arange(t)
    prow, pcol = pos // GRID_W, pos % GRID_W
    ang_r = prow.astype(F32)[:, None] * freqs[None, :]
    ang_c = pcol.astype(F32)[:, None] * freqs[None, :]
    cos = jnp.concatenate([jnp.cos(ang_r)] * 2 + [jnp.cos(ang_c)] * 2, axis=-1)
    sin = jnp.concatenate([-jnp.sin(ang_r), jnp.sin(ang_r), -jnp.sin(ang_c), jnp.sin(ang_c)], axis=-1)
    return jnp.tile(cos, (1, 2)), jnp.tile(sin, (1, 2))


def _na_kernel(q_ref, k_ref, v_ref, cos_ref, sin_ref, swap_ref, bias_ref, geo_ref, o_ref,
               qr_s, qp_s, kr_s, v_s, *, lc, rows, kbh, with_ctx):
    w = GRID_W
    nq, nk = NA_QROWS * w, kbh * w
    scale = HEAD_DIM ** -0.5

    def rope(x):
        return x * cos_ref[...] + _dot_exact_rhs(x, swap_ref[...], 2) * sin_ref[...]

    q_lat = q_ref[0, lc:, :]
    qr_s[...] = rope(q_lat).astype(BF16)
    qp_s[...] = q_ref[0].astype(BF16)
    kr_s[...] = rope(k_ref[0, lc:, :]).astype(BF16)
    v_s[...] = v_ref[0].astype(BF16)
    k_ctx = k_ref[0, :lc, :].astype(BF16)

    lane_a = lax.broadcasted_iota(jnp.int32, (nq, PAIR_W), 1) < HEAD_DIM
    head_sel = (lane_a, jnp.logical_not(lane_a))
    zero = jnp.zeros((), BF16)

    def attend(chains):
        scores = [[_dot_nt(q, k) * scale if b is None else _dot_nt(q, k) * scale + b
                   for q, k, b, _ in parts] for parts in chains]
        mx = [functools.reduce(jnp.maximum, [jnp.max(s, axis=-1, keepdims=True) for s in sc])
              for sc in scores]
        es = [[jnp.exp(s - m) for s in sc] for sc, m in zip(scores, mx)]
        den = [functools.reduce(lambda a, b: a + b, [jnp.sum(e, axis=-1, keepdims=True) for e in ee])
               for ee in es]
        acc = [functools.reduce(lambda a, b: a + b,
                                [_dot(e.astype(BF16), part[3]) for e, part in zip(ee, parts)])
               for ee, parts in zip(es, chains)]
        return [a / d for a, d in zip(acc, den)]

    def blocks(it, carry):
        chains, q0s = [], []
        v_ctx = v_s[0:lc, :]
        for j in range(NA_BLOCKS_PER_ITER):
            blk = it * NA_BLOCKS_PER_ITER + j
            r0 = blk * NA_QROWS
            kr = jnp.clip(r0 - min(NA_KH, rows) // 2, 0, rows - kbh)
            geo = geo_ref[blk]
            q0 = pl.multiple_of(r0 * w, w)
            k0 = pl.multiple_of(kr * w, w)
            qb = qr_s[pl.ds(q0, nq), :]
            qpb = qp_s[pl.ds(lc + q0, nq), :]
            kb = kr_s[pl.ds(k0, nk), :]
            vb = v_s[pl.ds(lc + k0, nk), :]
            q0s.append(q0)
            for hh in range(2):
                chains.append([(jnp.where(head_sel[hh], qb, zero), kb, bias_ref[0, geo, hh], vb),
                               (jnp.where(head_sel[hh], qpb, zero), k_ctx, None, v_ctx)])
        outs = attend(chains)
        for j, q0 in enumerate(q0s):
            o_ref[0, pl.ds(lc + q0, nq), :] = jnp.where(lane_a, outs[2 * j], outs[2 * j + 1])
        return carry

    lax.fori_loop(0, rows // NA_QROWS // NA_BLOCKS_PER_ITER, blocks, 0)

    if with_ctx:
        v_ctx = v_s[0:lc, :]
        nblk = lc // nq
        qcb = [qp_s[cblk * nq:(cblk + 1) * nq, :] for cblk in range(nblk)]
        outs = attend([[(jnp.where(head_sel[hh], qcb[cblk], zero), k_ctx, None, v_ctx)]
                       for cblk in range(nblk) for hh in range(2)])
        for cblk in range(nblk):
            o_ref[0, cblk * nq:(cblk + 1) * nq, :] = jnp.where(lane_a, outs[2 * cblk], outs[2 * cblk + 1])
    else:
        o_ref[0, 0:lc, :] = jnp.zeros((lc, PAIR_W), F32)


def _na(u_c, rpb, lc, with_ctx):
    b, r, fc = u_c.shape
    width = fc // 3
    npair = width // PAIR_W
    t = r - lc
    rows = t // GRID_W
    bias, deltas, geos = _na_bias_table(rpb, rows)
    kh, kbh, _, _ = _na_geometry(rows)
    geo_idx = jnp.asarray([geos.index(d) for d in deltas], jnp.int32)
    cos, sin = _rope_tables(t)
    quarter = HEAD_DIM // 4
    lane = np.arange(PAIR_W)
    src = np.where(lane % (2 * quarter) < quarter, lane + quarter, lane - quarter)
    swap = jnp.asarray((lane[:, None] == src[None, :]).astype(np.float32), BF16)
    nq, nk = NA_QROWS * GRID_W, kbh * GRID_W
    return pl.pallas_call(
        functools.partial(_na_kernel, lc=lc, rows=rows, kbh=kbh, with_ctx=with_ctx),
        grid=(npair, b),
        in_specs=[
            pl.BlockSpec((1, r, PAIR_W), lambda p, i: (i, 0, p)),
            pl.BlockSpec((1, r, PAIR_W), lambda p, i: (i, 0, npair + p)),
            pl.BlockSpec((1, r, PAIR_W), lambda p, i: (i, 0, 2 * npair + p)),
            pl.BlockSpec((t, PAIR_W), lambda p, i: (0, 0)),
            pl.BlockSpec((t, PAIR_W), lambda p, i: (0, 0)),
            pl.BlockSpec((PAIR_W, PAIR_W), lambda p, i: (0, 0)),
            pl.BlockSpec((1, len(geos), 2, nq, nk), lambda p, i: (p, 0, 0, 0, 0)),
            pl.BlockSpec(memory_space=pltpu.SMEM),
        ],
        out_specs=pl.BlockSpec((1, r, PAIR_W), lambda p, i: (i, 0, p)),
        out_shape=jax.ShapeDtypeStruct((b, r, width), F32),
        scratch_shapes=[
            pltpu.VMEM((t, PAIR_W), BF16),
            pltpu.VMEM((r, PAIR_W), BF16),
            pltpu.VMEM((t, PAIR_W), BF16),
            pltpu.VMEM((r, PAIR_W), BF16),
        ],
        compiler_params=_cparams(("parallel", "parallel")),
        name="natten",
    )(u_c, u_c, u_c, cos, sin, swap, bias, geo_idx)


def _merge_kernel(x_ref, ug_ref, yf_ref, yb_ref, eb_ref, eg_ref, hf_ref, hb_ref, gb_ref, yc_ref,
                  mod_ref, lw_ref, lb_ref, e_ref, et_ref, wbr_ref, wo_ref, g_ref, o_ref):
    width = yf_ref.shape[2]

    def group_mean(z):
        s = _dot_exact_rhs(z, e_ref[...], 2)
        return _dot_exact_rhs(s, et_ref[...], 2) * (1.0 / HEAD_DIM)

    y = yf_ref[0] + yb_ref[0]
    dev = y - group_mean(y)
    yn = dev * lax.rsqrt(group_mean(dev * dev) + LNX_EPS)
    y_a = (yn * lw_ref[...] + lb_ref[...] + eb_ref[0]) * eg_ref[0]
    y_b = (hf_ref[0] + hb_ref[0]) * jax.nn.gelu(gb_ref[0])
    ug = ug_ref[0]
    m = (_sigmoid(ug[:, 0:width]) * _dot(y_a.astype(BF16), wbr_ref[0])
         + _sigmoid(ug[:, width:2 * width]) * _dot(y_b.astype(BF16), wbr_ref[1])
         + _sigmoid(ug[:, 2 * width:3 * width]) * _dot(yc_ref[0].astype(BF16), wbr_ref[2]))
    out = _dot(m.astype(BF16), wo_ref[...])
    o_ref[0] = x_ref[0] + mod_ref[0, 0, 2:3, :] * _rms(out, g_ref[...])


def _merge(xs, u_g, u_b, wkv_out, lru_out, y_c, mod, p, ng1, lc, skip_ctx):
    b, r, d = xs.shape
    width = y_c.shape[2]
    tm = ROW_TILE
    off = lc // tm if skip_ctx else 0
    nt = r // tm - off
    nctx = lc // tm
    heads = width // HEAD_DIM
    hid = np.arange(width) // HEAD_DIM
    e = (hid[:, None] == np.arange(V7X_LANES)[None, :]).astype(np.float32)
    e_bf, et_bf = jnp.asarray(e, BF16), jnp.asarray(e.T, BF16)
    assert heads <= V7X_LANES
    rowblk = lambda i, t: (i, t + off, 0)
    const2 = lambda i, t: (0, 0)
    act = lambda wd: pl.BlockSpec((1, tm, wd), rowblk)
    return pl.pallas_call(
        _merge_kernel,
        grid=(b, nt),
        in_specs=[
            act(d), act(3 * width), act(width), act(width), act(width), act(width), act(width), act(width),
            pl.BlockSpec((1, tm, width), lambda i, t: (i, t + off, 1)),
            act(width),
            pl.BlockSpec((1, 1, 6, d), lambda i, t: (i, jnp.where(t + off < nctx, 0, 1), 0, 0)),
            pl.BlockSpec((1, width), const2),
            pl.BlockSpec((1, width), const2),
            pl.BlockSpec((width, V7X_LANES), const2),
            pl.BlockSpec((V7X_LANES, width), const2),
            pl.BlockSpec((3, width, d), lambda i, t: (0, 0, 0)),
            pl.BlockSpec((d, d), const2),
            pl.BlockSpec((1, d), const2),
        ],
        out_specs=pl.BlockSpec((1, tm, d), lambda i, t: (i, t, 0)),
        out_shape=jax.ShapeDtypeStruct((b, nt * tm, d), F32),
        compiler_params=_cparams(("parallel", "parallel")),
        name="merge",
    )(xs, u_g, wkv_out[0], wkv_out[1], wkv_out[2], wkv_out[3], lru_out[0], lru_out[1], u_b, y_c,
      mod, p["rwkv_lnx_w"].reshape(1, -1), p["rwkv_lnx_b"].reshape(1, -1), e_bf, et_bf,
      p["w_branch"].astype(BF16), p["w_out"].astype(BF16), ng1)


def _ffn_kernel(x_ref, xp_ref, xn_ref, mod_ref, g2_ref, g3_ref, wi_ref, cw_ref, cb_ref, wo_ref, o_ref,
                *, tm, nctx, nt, dff):
    t = pl.program_id(1)
    first = jnp.logical_or(t == 0, t == nctx)
    last = jnp.logical_or(t == nctx - 1, t == nt - 1)
    sh, sc, gate_mod = mod_ref[0, 0, 3:4, :], mod_ref[0, 0, 4:5, :], mod_ref[0, 0, 5:6, :]

    def hidden(xv):
        return (_rms(xv, g2_ref[...]) * (1.0 + sc) + sh).astype(BF16)

    x = x_ref[0]
    h = hidden(x)
    gate = _dot(h, wi_ref[:, 0:dff])
    up = _dot(h, wi_ref[:, dff:2 * dff])
    g_prev = _dot(hidden(xp_ref[0]), wi_ref[:, 0:dff])[FFN_HALO - 1:FFN_HALO, :]
    g_next = _dot(hidden(xn_ref[0]), wi_ref[:, 0:dff])[0:1, :]
    g_prev = jnp.where(first, 0.0, g_prev)
    g_next = jnp.where(last, 0.0, g_next)
    rows = lax.broadcasted_iota(jnp.int32, (tm, 1), 0)
    gate_m1 = jnp.where(rows == 0, g_prev, pltpu.roll(gate, 1, 0))
    gate_p1 = jnp.where(rows == tm - 1, g_next, pltpu.roll(gate, tm - 1, 0))
    gc = cw_ref[0:1, :] * gate_m1 + cw_ref[1:2, :] * gate + cw_ref[2:3, :] * gate_p1 + cb_ref[...]
    act = (gc * _sigmoid(gc) * up).astype(BF16)
    out = _dot(act, wo_ref[...])
    o_ref[0] = x + gate_mod * _rms(out, g3_ref[...])


def _ffn(x1, mod, p, ng2, ng3, lc):
    b, r, d = x1.shape
    tm = ROW_TILE
    nt = r // tm
    nctx = lc // tm
    hb = tm // FFN_HALO
    nhalo = r // FFN_HALO
    dff = p["ffn_w_out"].shape[0]
    const2 = lambda i, t: (0, 0)
    return pl.pallas_call(
        functools.partial(_ffn_kernel, tm=tm, nctx=nctx, nt=nt, dff=dff),
        grid=(b, nt),
        in_specs=[
            pl.BlockSpec((1, tm, d), lambda i, t: (i, t, 0)),
            pl.BlockSpec((1, FFN_HALO, d), lambda i, t: (i, jnp.maximum(t * hb - 1, 0), 0)),
            pl.BlockSpec((1, FFN_HALO, d), lambda i, t: (i, jnp.minimum((t + 1) * hb, nhalo - 1), 0)),
            pl.BlockSpec((1, 1, 6, d), lambda i, t: (i, jnp.where(t < nctx, 0, 1), 0, 0)),
            pl.BlockSpec((1, d), const2),
            pl.BlockSpec((1, d), const2),
            pl.BlockSpec((d, 2 * dff), const2),
            pl.BlockSpec((3, dff), const2),
            pl.BlockSpec((1, dff), const2),
            pl.BlockSpec((dff, d), const2),
        ],
        out_specs=pl.BlockSpec((1, tm, d), lambda i, t: (i, t, 0)),
        out_shape=jax.ShapeDtypeStruct((b, r, d), F32),
        compiler_params=_cparams(("parallel", "parallel")),
        name="conv_ffn",
    )(x1, x1, x1, mod, ng2, ng3, p["ffn_w_in"].astype(BF16), p["ffn_conv_w"],
      p["ffn_conv_b"].reshape(1, -1), p["ffn_w_out"].astype(BF16))


def _in_tile(r):
    for parts in (4, 8, 2, 1, 16, 32):
        if r % parts == 0 and (r // parts) % V7X_SUBLANES == 0 and r // parts <= 1152:
            return r // parts
    raise ValueError(f"no row tile for {r} rows")


def _col_tile(n):
    for parts in (1, 2, 3, 4, 6, 8):
        if n % parts == 0 and (n // parts) % V7X_LANES == 0 and n // parts <= 1792:
            return n // parts
    raise ValueError(f"no column tile for {n} columns")


def _layer(xs, mod, p, lc, update_ctx):
    width = p["rwkv_k_k"].shape[0]
    a_cols = 3 * width + W_LORA + A_LORA + G_LORA
    bounds = np.cumsum([0, a_cols, 2 * width, 3 * width, 3 * width])
    ng = p["norm_g"]
    w_in = p["w_in"].astype(BF16)
    tm = _in_tile(xs.shape[1])
    u = [_in_proj(xs, mod, ng[0:1], w_in[:, lo:hi], lc, tm, _col_tile(int(hi - lo)))
         for lo, hi in zip(bounds[:-1], bounds[1:])]
    u_a, u_b, u_c, u_g = u
    wkv_out = _wkv(u_a, p, lc)
    lru_out = _lru(u_b, p, lc)
    y_c = _na(u_c, p["na_rpb"], lc, update_ctx)
    x1 = _merge(xs, u_g, u_b, wkv_out, lru_out, y_c, mod, p, ng[1:2], lc, not update_ctx)
    return _ffn(x1, mod, p, ng[2:3], ng[3:4], lc if update_ctx else 0)


_PARAM_NAMES = ("norm_g", "w_in", "rwkv_mu", "rwkv_w0", "rwkv_w_up", "rwkv_a0", "rwkv_a_up", "rwkv_g_up",
                "rwkv_k_k", "rwkv_k_a", "rwkv_r_k", "rwkv_lnx_w", "rwkv_lnx_b", "lru_conv_w", "lru_conv_b",
                "lru_gate_a_w", "lru_gate_a_b", "lru_gate_x_w", "lru_gate_x_b", "lru_lambda", "na_rpb",
                "w_branch", "w_out", "ffn_w_in", "ffn_conv_w", "ffn_conv_b", "ffn_w_out")


def kernel(x, c, ctx, c_ctx, ada_w, ada_b, norm_g, w_in, rwkv_mu, rwkv_w0, rwkv_w_up, rwkv_a0, rwkv_a_up, rwkv_g_up, rwkv_k_k, rwkv_k_a, rwkv_r_k, rwkv_lnx_w, rwkv_lnx_b, lru_conv_w, lru_conv_b, lru_gate_a_w, lru_gate_a_b, lru_gate_x_w, lru_gate_x_b, lru_lambda, na_rpb, w_branch, w_out, ffn_w_in, ffn_conv_w, ffn_conv_b, ffn_w_out):
    stacked = dict(zip(_PARAM_NAMES, (norm_g, w_in, rwkv_mu, rwkv_w0, rwkv_w_up, rwkv_a0, rwkv_a_up,
                                      rwkv_g_up, rwkv_k_k, rwkv_k_a, rwkv_r_k, rwkv_lnx_w, rwkv_lnx_b,
                                      lru_conv_w, lru_conv_b, lru_gate_a_w, lru_gate_a_b, lru_gate_x_w,
                                      lru_gate_x_b, lru_lambda, na_rpb, w_branch, w_out, ffn_w_in,
                                      ffn_conv_w, ffn_conv_b, ffn_w_out)))
    depth = ada_w.shape[0]
    b, _, d = x.shape
    lc = ctx.shape[1]
    pad = (-(b + 1)) % V7X_SUBLANES
    cc = jnp.concatenate([c, c_ctx[None, :], jnp.zeros((pad, d), c.dtype)], axis=0)
    mod_all = _modulation(cc, ada_w, ada_b)
    xs = jnp.concatenate([ctx, x], axis=1)
    for l in range(depth):
        p = {name: val[l] for name, val in stacked.items()}
        p["rwkv_r_k"] = p["rwkv_r_k"].reshape(-1)
        m = mod_all[l].reshape(-1, 6, d)
        mod = jnp.stack([jnp.broadcast_to(m[b], (b, 6, d)), m[:b]], axis=1)
        xs = _layer(xs, mod, p, lc, l < depth - 1)
    return xs
```

```python
import functools

import numpy as np
import jax
import jax.numpy as jnp
from jax import lax
from jax.experimental import pallas as pl
from jax.experimental.pallas import tpu as pltpu

F32 = jnp.float32
BF16 = jnp.bfloat16
ACT_DTYPE = BF16

GRID_W = 64
HEAD_DIM = 64
W_LORA = 64
A_LORA = 64
G_LORA = 128
LNX_EPS = 64e-5
LRU_C = 8.0
NA_KH = 8
NA_KW = 16
NA_QROWS = 2
NA_BLOCKS_PER_ITER = 4
NA_BAND_ROWS = 10
ROPE_BASE = 10000.0
EPS = 1e-6
NEG_INF = -1e30

V7X_LANES = 128
V7X_SUBLANES = 8
V7X_VMEM_LIMIT_BYTES = 56 * 1024 * 1024

PAIR_W = 2 * HEAD_DIM
SCAN_CHUNK = 64
WKV_DECAY_SCALE = float(np.exp(-0.5))
ROW_TILE = 256
LRU_GROUP = 256
IN_HALO = 16
FFN_BATCHES = 4
FFN_HALO = 16


def _cparams(sem):
    return pltpu.CompilerParams(dimension_semantics=sem, vmem_limit_bytes=V7X_VMEM_LIMIT_BYTES)


def _split_bf16(x, parts):
    out = []
    rem = x
    for _ in range(parts - 1):
        hi = rem.astype(BF16)
        out.append(hi)
        rem = rem - hi.astype(F32)
    out.append(rem.astype(BF16))
    return out


def _dot(a, b):
    return jnp.dot(a, b, preferred_element_type=F32)


def _dot_nt(a, b):
    return lax.dot_general(a, b, (((1,), (1,)), ((), ())), preferred_element_type=F32)


def _dot_tn(a, b):
    return lax.dot_general(a, b, (((0,), (0,)), ((), ())), preferred_element_type=F32)


def _dot_exact_lhs(a_bf16, x, parts):
    acc = None
    for piece in _split_bf16(x, parts):
        t = _dot(a_bf16, piece)
        acc = t if acc is None else acc + t
    return acc


def _dot_exact_rhs(x, b_bf16, parts):
    acc = None
    for piece in _split_bf16(x, parts):
        t = _dot(piece, b_bf16)
        acc = t if acc is None else acc + t
    return acc


def _sigmoid(x):
    return 1.0 / (1.0 + jnp.exp(-x))


def _softplus(x):
    return jnp.maximum(x, 0.0) + jnp.log(1.0 + jnp.exp(-jnp.abs(x)))


def _rms(x, g):
    return x * lax.rsqrt(jnp.mean(x * x, axis=-1, keepdims=True) + EPS) * g


def _mod_kernel(c_ref, w_ref, b_ref, o_ref):
    c = c_ref[...]
    s = c * _sigmoid(c)
    o_ref[0] = _dot(s.astype(BF16), w_ref[0].astype(BF16)) + b_ref[0]


def _modulation(cc, ada_w, ada_b):
    depth, d, n = ada_w.shape
    rows = cc.shape[0]
    tn = n // 4
    return pl.pallas_call(
        _mod_kernel,
        grid=(depth, n // tn),
        in_specs=[
            pl.BlockSpec((rows, d), lambda l, j: (0, 0)),
            pl.BlockSpec((1, d, tn), lambda l, j: (l, 0, j)),
            pl.BlockSpec((1, 1, tn), lambda l, j: (l, 0, j)),
        ],
        out_specs=pl.BlockSpec((1, rows, tn), lambda l, j: (l, 0, j)),
        out_shape=jax.ShapeDtypeStruct((depth, rows, n), F32),
        compiler_params=_cparams(("parallel", "parallel")),
        name="modulation",
    )(cc, ada_w, ada_b.reshape(depth, 1, n))


def _in_kernel(x_ref, mod_ref, g_ref, w_ref, o_ref, hb_ref, *, lc, tm):
    @pl.when(pl.program_id(2) == 0)
    def _():
        y = _rms(x_ref[0], g_ref[...])
        row = pl.program_id(1) * tm + lax.broadcasted_iota(jnp.int32, (tm, 1), 0)
        is_ctx = row < lc
        sh = jnp.where(is_ctx, mod_ref[0, 0, 0:1, :], mod_ref[0, 1, 0:1, :])
        sc = jnp.where(is_ctx, mod_ref[0, 0, 1:2, :], mod_ref[0, 1, 1:2, :])
        hb_ref[...] = (y * (1.0 + sc) + sh).astype(BF16)

    o_ref[0] = _dot(hb_ref[...], w_ref[...]).astype(o_ref.dtype)


def _in_taps_kernel(x_ref, xp_ref, xn_ref, mod_ref, g_ref, w_ref, taps_ref, o_ref, hb_ref,
                    *, lc, r, tm, shifts):
    t = pl.program_id(1)
    halo = IN_HALO
    ns = V7X_SUBLANES

    @pl.when(pl.program_id(2) == 0)
    def _():
        def hidden(xv, row0):
            row = row0 + lax.broadcasted_iota(jnp.int32, (xv.shape[0], 1), 0)
            is_ctx = row < lc
            sh = jnp.where(is_ctx, mod_ref[0, 0, 0:1, :], mod_ref[0, 1, 0:1, :])
            sc = jnp.where(is_ctx, mod_ref[0, 0, 1:2, :], mod_ref[0, 1, 1:2, :])
            return (_rms(xv, g_ref[...]) * (1.0 + sc) + sh).astype(BF16)

        hb_ref[0:halo, :] = hidden(xp_ref[0], t * tm - halo)
        hb_ref[halo:halo + tm, :] = hidden(x_ref[0], t * tm)
        hb_ref[halo + tm:2 * halo + tm, :] = hidden(xn_ref[0], (t + 1) * tm)

    tn = o_ref.shape[2]
    lead = halo // ns
    sub_row = lax.broadcasted_iota(jnp.int32, (1, ns, 1), 1)
    tiles = tm // ns
    u = _dot(hb_ref[...], w_ref[...]).reshape(tiles + 2 * lead, ns, tn)
    row = (t * tm + ns * lax.broadcasted_iota(jnp.int32, (tiles, 1, 1), 0)
           + lax.broadcasted_iota(jnp.int32, (1, ns, 1), 1))
    from_start = row - jnp.where(row < lc, 0, lc)
    to_end = jnp.where(row < lc, lc, r) - 1 - row
    acc = None
    for j, s in enumerate(shifts):
        coef = taps_ref[j:j + 1, :].reshape(1, 1, tn)
        if s == 0:
            term = u[lead:lead + tiles]
        elif s < 0:
            rolled = pltpu.roll(u[lead - 1:lead + tiles], -s, 1)
            term = jnp.where(sub_row >= -s, rolled[1:], rolled[:-1])
            term = jnp.where(from_start >= -s, term, 0.0)
        else:
            rolled = pltpu.roll(u[lead:lead + tiles + 1], ns - s, 1)
            term = jnp.where(sub_row < ns - s, rolled[:-1], rolled[1:])
            term = jnp.where(to_end >= s, term, 0.0)
        acc = coef * term if acc is None else acc + coef * term
    acc = acc + taps_ref[len(shifts):len(shifts) + 1, :].reshape(1, 1, tn)
    o_ref[0] = acc.reshape(tm, tn).astype(o_ref.dtype)


def _in_proj_taps(xs, mod, g, w, taps, shifts, lc, tm, tn):
    b, r, d = xs.shape
    n = w.shape[1]
    hb = tm // IN_HALO
    nhalo = r // IN_HALO
    return pl.pallas_call(
        functools.partial(_in_taps_kernel, lc=lc, r=r, tm=tm, shifts=shifts),
        grid=(b, r // tm, n // tn),
        in_specs=[
            pl.BlockSpec((1, tm, d), lambda i, t, j: (i, t, 0)),
            pl.BlockSpec((1, IN_HALO, d), lambda i, t, j: (i, jnp.maximum(t * hb - 1, 0), 0)),
            pl.BlockSpec((1, IN_HALO, d), lambda i, t, j: (i, jnp.minimum((t + 1) * hb, nhalo - 1), 0)),
            pl.BlockSpec((1, 2, 6, d), lambda i, t, j: (i, 0, 0, 0)),
            pl.BlockSpec((1, d), lambda i, t, j: (0, 0)),
            pl.BlockSpec((d, tn), lambda i, t, j: (0, j)),
            pl.BlockSpec((len(shifts) + 1, tn), lambda i, t, j: (0, j)),
        ],
        out_specs=pl.BlockSpec((1, tm, tn), lambda i, t, j: (i, t, j)),
        out_shape=jax.ShapeDtypeStruct((b, r, n), F32),
        scratch_shapes=[pltpu.VMEM((tm + 2 * IN_HALO, d), BF16)],
        compiler_params=_cparams(("parallel", "parallel", "arbitrary")),
        name="in_proj_taps",
    )(xs, xs, xs, mod, g, w, taps)


def _in_proj(xs, mod, g, w, lc, tm, tn):
    b, r, d = xs.shape
    n = w.shape[1]
    return pl.pallas_call(
        functools.partial(_in_kernel, lc=lc, tm=tm),
        grid=(b, r // tm, n // tn),
        in_specs=[
            pl.BlockSpec((1, tm, d), lambda i, t, j: (i, t, 0)),
            pl.BlockSpec((1, 2, 6, d), lambda i, t, j: (i, 0, 0, 0)),
            pl.BlockSpec((1, d), lambda i, t, j: (0, 0)),
            pl.BlockSpec((d, tn), lambda i, t, j: (0, j)),
        ],
        out_specs=pl.BlockSpec((1, tm, tn), lambda i, t, j: (i, t, j)),
        out_shape=jax.ShapeDtypeStruct((b, r, n), ACT_DTYPE),
        scratch_shapes=[pltpu.VMEM((tm, d), BF16)],
        compiler_params=_cparams(("parallel", "parallel", "arbitrary")),
        name="in_proj",
    )(xs, mod, g, w)


def _stack_pair(x, lane_a):
    return jnp.concatenate([jnp.where(lane_a, x, 0.0), jnp.where(lane_a, 0.0, x)], axis=0)


def _wkv_chains(chains, c):
    n = 2 * c
    lane_a = lax.broadcasted_iota(jnp.int32, (c, PAIR_W), 1) < HEAD_DIM
    stack = lambda x: _stack_pair(x.astype(BF16), lane_a)
    nch = len(chains)

    ops = []
    for ch in chains:
        r, kd, v, kkn, a, lw, cum = ch[:7]
        tot = jnp.sum(lw, axis=0, keepdims=True)
        e_inv = jnp.exp(-cum)
        e_rem = jnp.exp(tot - cum)
        akk = a * kkn
        lhs = jnp.concatenate([stack(-kkn * jnp.exp(cum - lw)), stack(r * jnp.exp(cum))], axis=0)
        bk = jnp.concatenate([stack(akk * e_inv), stack(kd * e_inv)], axis=0)
        bkc = jnp.concatenate([stack(akk * e_rem), stack(kd * e_rem)], axis=0)
        ops.append((lhs, bk, bkc, stack(v), tot))

    g = [_dot_nt(o[0], o[1]) for o in ops]
    ls = [_dot_nt(o[0], ch[7].astype(BF16)) for o, ch in zip(ops, chains)]
    m_b = [jnp.where(ch[8], gi[:n, :n], 0.0) for ch, gi in zip(chains, g)]
    m_k = [jnp.where(ch[8], gi[:n, n:], 0.0).astype(BF16) for ch, gi in zip(chains, g)]
    n_bk = [jnp.concatenate([jnp.where(ch[9], gi[n:, :n], 0.0).astype(BF16),
                             jnp.where(ch[9], gi[n:, n:], 0.0).astype(BF16)], axis=1)
            for ch, gi in zip(chains, g)]
    mkv = [_dot(mk, o[3]) for mk, o in zip(m_k, ops)]

    eye = (lax.broadcasted_iota(jnp.int32, (n, n), 0) == lax.broadcasted_iota(jnp.int32, (n, n), 1))
    t_inv = [jnp.where(eye, 1.0, 0.0) + m for m in m_b]
    m_pow = [m.astype(BF16) for m in m_b]
    span = 1
    while 2 * span < c:
        m_pow = [_dot(mp, mp).astype(BF16) for mp in m_pow]
        t_inv = [t + _dot(t.astype(BF16), mp) for t, mp in zip(t_inv, m_pow)]
        span *= 2

    u = [_dot(t.astype(BF16), (l[:n] + mv).astype(BF16)).astype(BF16)
         for t, l, mv in zip(t_inv, ls, mkv)]
    uv = [jnp.concatenate([ui, o[3]], axis=0) for ui, o in zip(u, ops)]
    ys = [l[n:] + _dot(nb, x) for l, nb, x in zip(ls, n_bk, uv)]
    s_new = [ch[7] * jnp.exp(o[4]) + _dot_tn(x, o[2]) for ch, o, x in zip(chains, ops, uv)]
    return [(ys[j][:c] + ys[j][c:], s_new[j]) for j in range(nch)]


def _scan_kernel(uf_ref, ub_ref, xf_ref, xb_ref,
                 w0_ref, wup_ref, a0_ref, aup_ref, gup_ref, kk_ref, ka_ref, rk_ref,
                 tri_ref, ms_ref, mi_ref, bd_ref,
                 wa_ref, ba_ref, wx_ref, bx_ref, lam_ref,
                 yf_ref, yb_ref, eb_ref, eg_ref, hf_ref, hb_ref, s_ref, h_ref, *, c):
    width = yf_ref.shape[2]
    npair = width // PAIR_W

    @pl.when(pl.program_id(1) == 0)
    def _():
        s_ref[...] = jnp.zeros(s_ref.shape, F32)
        h_ref[...] = jnp.zeros(h_ref.shape, F32)

    lru_gates = [_lru_gates(x_ref, wa_ref, wx_ref, d) for d, x_ref in enumerate((xf_ref, xb_ref))]

    us = (uf_ref[0], ub_ref[0])
    o = 3 * width
    w_lo = [jnp.tanh(u[:, o:o + W_LORA]).astype(BF16) for u in us]
    a_lo = [u[:, o + W_LORA:o + W_LORA + A_LORA].astype(BF16) for u in us]
    wpre = [w0_ref[d:d + 1, :] + _dot(w_lo[d], wup_ref[d]) for d in range(2)]
    apre = [a0_ref[d:d + 1, :] + _dot(a_lo[d], aup_ref[d]) for d in range(2)]
    apre_rev = a0_ref[1:2, :] + _dot(a_lo[0], aup_ref[1])
    g_lo = _sigmoid(us[0][:, o + W_LORA + A_LORA:o + W_LORA + A_LORA + G_LORA]).astype(BF16)
    eg_ref[0] = _dot(g_lo, gup_ref[...])

    bd = bd_ref[...]
    masks = [(ms_ref[d] > 0.5, mi_ref[d] > 0.5) for d in range(2)]
    def cols(d, part, p):
        return us[d][:, part * width + p * PAIR_W:part * width + (p + 1) * PAIR_W]

    lw, cum, kkraw, ss = {}, {}, {}, {}
    for p in range(npair):
        sl = slice(p * PAIR_W, (p + 1) * PAIR_W)
        for d in range(2):
            lw[d, p] = -WKV_DECAY_SCALE * _sigmoid(wpre[d][:, sl])
            cum[d, p] = _dot_exact_lhs(tri_ref[d], lw[d, p], 2)
            kkraw[d, p] = cols(d, 1, p) * kk_ref[:, sl]
            ss[d, p] = _dot((kkraw[d, p] * kkraw[d, p]).astype(BF16), bd)

    keys = [(d, p) for p in range(npair) for d in range(2)]
    chains, rkk = [], []
    for d, p in keys:
        sl = slice(p * PAIR_W, (p + 1) * PAIR_W)
        r, k, v = cols(d, 0, p), cols(d, 1, p), cols(d, 2, p)
        a = _sigmoid(apre[d][:, sl])
        kd = k * (1.0 + (a - 1.0) * ka_ref[:, sl])
        kkn = kkraw[d, p] * lax.rsqrt(jnp.maximum(ss[d, p], 1e-24))
        chains.append((r, kd, v, kkn, a, lw[d, p], cum[d, p], s_ref[d, p]) + masks[d])
        if d == 0:
            kd_rev = k * (1.0 + (_sigmoid(apre_rev[:, sl]) - 1.0) * ka_ref[:, sl])
            rkk.append(_split_bf16(r * (kd + kd_rev) * rk_ref[:, sl], 2))

    rk_hi = [_dot(s[0], bd) for s in rkk]
    rk_lo = [_dot(s[1], bd) for s in rkk]
    for (d, p), (y, s_new) in zip(keys, _wkv_chains(chains, c)):
        s_ref[d, p] = s_new
        (yf_ref, yb_ref)[d][0, :, p * PAIR_W:(p + 1) * PAIR_W] = y
    for p in range(npair):
        eb_ref[0, :, p * PAIR_W:(p + 1) * PAIR_W] = (rk_hi[p] + rk_lo[p]) * cols(0, 2, p)

    for d, (x_ref, out_ref) in enumerate(((xf_ref, hf_ref), (xb_ref, hb_ref))):
        _lru_step(x_ref, lru_gates[d], ba_ref[d:d + 1, :], bx_ref[d:d + 1, :], lam_ref, out_ref, h_ref, c, d)


def _wkv_consts(c):
    t = np.arange(c)
    lower = (t[:, None] >= t[None, :])
    tri = np.stack([lower, lower.T]).astype(np.float32)
    strict = np.stack([t[:, None] > t[None, :], t[:, None] < t[None, :]])
    incl = np.stack([lower, lower.T])
    ms = np.tile(strict, (1, 2, 2)).astype(np.float32)
    mi = np.tile(incl, (1, 2, 2)).astype(np.float32)
    h = np.arange(PAIR_W) // HEAD_DIM
    bd = (h[:, None] == h[None, :]).astype(np.float32)
    return jnp.asarray(tri, BF16), jnp.asarray(ms), jnp.asarray(mi), jnp.asarray(bd, BF16)


def _scans(u_a, u_b, p, lc):
    b, r, fa = u_a.shape
    width = p["rwkv_k_k"].shape[0]
    c = SCAN_CHUNK
    nc, nl = lc // c, (r - lc) // c
    nchunk = nc + nl
    npair = width // PAIR_W
    ng = width // LRU_GROUP
    tri, ms, mi, bd = _wkv_consts(c)
    const4 = lambda bi, i: (0, 0, 0, 0)

    def rev(i):
        return jnp.where(i < nc, nc - 1 - i, 2 * nc + nl - 1 - i)

    cur_f = lambda bi, i: (bi, i, 0)
    cur_b = lambda bi, i: (bi, rev(i), 0)
    const2 = lambda bi, i: (0, 0)
    const3 = lambda bi, i: (0, 0, 0)

    row = lambda x: x.reshape(1, -1)
    out_sd = jax.ShapeDtypeStruct((b, r, width), F32)
    outs = pl.pallas_call(
        functools.partial(_scan_kernel, c=c),
        grid=(b, nchunk),
        in_specs=[
            pl.BlockSpec((1, c, fa), cur_f),
            pl.BlockSpec((1, c, fa), cur_b),
            pl.BlockSpec((1, c, width), cur_f),
            pl.BlockSpec((1, c, width), cur_b),
            pl.BlockSpec((2, width), const2),
            pl.BlockSpec((2, W_LORA, width), const3),
            pl.BlockSpec((2, width), const2),
            pl.BlockSpec((2, A_LORA, width), const3),
            pl.BlockSpec((G_LORA, width), const2),
            pl.BlockSpec((1, width), const2),
            pl.BlockSpec((1, width), const2),
            pl.BlockSpec((1, width), const2),
            pl.BlockSpec((2, c, c), const3),
            pl.BlockSpec((2, 2 * c, 2 * c), const3),
            pl.BlockSpec((2, 2 * c, 2 * c), const3),
            pl.BlockSpec((PAIR_W, PAIR_W), const2),
            pl.BlockSpec((2, ng, LRU_GROUP, LRU_GROUP), const4),
            pl.BlockSpec((2, width), const2),
            pl.BlockSpec((2, ng, LRU_GROUP, LRU_GROUP), const4),
            pl.BlockSpec((2, width), const2),
            pl.BlockSpec((2, width), const2),
        ],
        out_specs=[
            pl.BlockSpec((1, c, width), cur_f),
            pl.BlockSpec((1, c, width), cur_b),
            pl.BlockSpec((1, c, width), cur_f),
            pl.BlockSpec((1, c, width), cur_f),
            pl.BlockSpec((1, c, width), cur_f),
            pl.BlockSpec((1, c, width), cur_b),
        ],
        out_shape=[out_sd] * 6,
        scratch_shapes=[
            pltpu.VMEM((2, npair, PAIR_W, PAIR_W), F32),
            pltpu.VMEM((2, width), F32),
        ],
        compiler_params=_cparams(("parallel", "arbitrary")),
        name="scans",
    )(u_a, u_a, u_b, u_b,
      p["rwkv_w0"], p["rwkv_w_up"].astype(BF16), p["rwkv_a0"],
      p["rwkv_a_up"].astype(BF16), p["rwkv_g_up"].astype(BF16),
      row(p["rwkv_k_k"]), row(p["rwkv_k_a"]), row(p["rwkv_r_k"]),
      tri, ms, mi, bd,
      _block_diag_groups(p["lru_gate_a_w"]), p["lru_gate_a_b"],
      _block_diag_groups(p["lru_gate_x_w"]), p["lru_gate_x_b"], p["lru_lambda"])
    return outs[:4], outs[4:]


def _lru_scan(a, bv, h0, out_ref, c, reverse):
    ns = V7X_SUBLANES
    tiles = c // ns
    width = a.shape[1]
    a = a.reshape(tiles, ns, width)
    bv = bv.reshape(tiles, ns, width)
    sub_row = lax.broadcasted_iota(jnp.int32, (1, ns, 1), 1)
    s = 1
    while s < ns:
        keep = sub_row < ns - s if reverse else sub_row >= s
        shift = ns - s if reverse else s
        a_sh = jnp.where(keep, pltpu.roll(a, shift, 1), 1.0)
        b_sh = jnp.where(keep, pltpu.roll(bv, shift, 1), 0.0)
        bv = a * b_sh + bv
        a = a * a_sh
        s *= 2
    state = h0
    for k in (range(tiles - 1, -1, -1) if reverse else range(tiles)):
        hk = bv[k] + a[k] * state
        out_ref[0, k * ns:(k + 1) * ns, :] = hk
        state = hk[0:1, :] if reverse else hk[ns - 1:ns, :]
    return state


def _lru_gates(u_ref, wa_ref, wx_ref, d):
    xcb = u_ref[0].astype(BF16)
    ng = xcb.shape[1] // LRU_GROUP
    ga = jnp.concatenate([_dot(xcb[:, g * LRU_GROUP:(g + 1) * LRU_GROUP], wa_ref[d, g])
                          for g in range(ng)], axis=1)
    gx = jnp.concatenate([_dot(xcb[:, g * LRU_GROUP:(g + 1) * LRU_GROUP], wx_ref[d, g])
                          for g in range(ng)], axis=1)
    return ga, gx


def _lru_step(u_ref, gates, bias_a, bias_x, lam_ref, out_ref, h_ref, c, d):
    ga, gx = gates
    rg = _sigmoid(ga + bias_a)
    ig = _sigmoid(gx + bias_x)
    a = jnp.exp(-LRU_C * rg * _softplus(-lam_ref[d:d + 1, :]))
    bv = jnp.sqrt(1.0 - a * a) * (ig * u_ref[0])
    h_ref[d:d + 1, :] = _lru_scan(a, bv, h_ref[d:d + 1, :], out_ref, c, d == 1)


def _block_diag_groups(w):
    nd, nb, bw, _ = w.shape
    per = LRU_GROUP // bw
    w = w.reshape(nd, nb // per, per, bw, bw)
    eye = jnp.eye(per, dtype=w.dtype)
    out = jnp.einsum("dgpij,pq->dgpiqj", w, eye)
    return out.reshape(nd, nb // per, LRU_GROUP, LRU_GROUP).astype(BF16)


def _na_geometry(rows):
    assert rows >= NA_BAND_ROWS
    kh = min(NA_KH, rows)
    kbh = NA_BAND_ROWS
    deltas = []
    for blk in range(rows // NA_QROWS):
        r0 = blk * NA_QROWS
        kr = int(np.clip(r0 - kh // 2, 0, rows - kbh))
        deltas.append(r0 - kr)
    return kh, kbh, deltas, sorted(set(deltas))


def _na_bias_table(rpb, rows):
    w = GRID_W
    kh, kbh, deltas, geos = _na_geometry(rows)
    nq, nk = NA_QROWS * w, kbh * w
    drow = np.zeros((len(geos), nq, nk), np.int32)
    dcol = np.zeros((len(geos), nq, nk), np.int32)
    ok = np.zeros((len(geos), nq, nk), bool)
    qc = np.tile(np.arange(w), NA_QROWS)
    qr = np.repeat(np.arange(NA_QROWS), w)
    kc = np.tile(np.arange(w), kbh)
    kj = np.repeat(np.arange(kbh), w)
    cs = np.clip(qc - NA_KW // 2, 0, w - NA_KW)
    col_ok = (kc[None, :] >= cs[:, None]) & (kc[None, :] < cs[:, None] + NA_KW)
    dc = np.clip(kc[None, :] - qc[:, None] + NA_KW - 1, 0, 2 * NA_KW - 2)
    for gi, delta in enumerate(geos):
        seen = None
        for blk, dl in enumerate(deltas):
            if dl != delta:
                continue
            r0 = blk * NA_QROWS
            kr = r0 - delta
            qrow = r0 + qr
            krow = kr + kj
            rs = np.clip(qrow - kh // 2, 0, rows - kh)
            row_ok = (krow[None, :] >= rs[:, None]) & (krow[None, :] < rs[:, None] + kh)
            dr = np.clip(krow[None, :] - qrow[:, None] + NA_KH - 1, 0, 2 * NA_KH - 2)
            cur = (row_ok & col_ok, dr)
            if seen is None:
                seen = cur
            else:
                assert np.array_equal(seen[0], cur[0]) and np.array_equal(seen[1], cur[1])
        ok[gi], drow[gi], dcol[gi] = seen[0], seen[1], dc
    dr5 = drow.reshape(len(geos), NA_QROWS, w, kbh, w)[:, :, 0, :, 0]
    dc4 = dc.reshape(NA_QROWS, w, kbh, w)[0, :, 0, :]
    oh_r = (dr5[..., None] == np.arange(rpb.shape[1])).astype(np.float32)
    oh_c = (dc4[None] == np.arange(rpb.shape[2])[:, None, None]).astype(np.float32)
    rows_sel = jnp.einsum("gqjr,hrc->hgqjc", oh_r, rpb, precision=lax.Precision.HIGHEST)
    bias = jnp.einsum("hgqjc,cxy->hgqxjy", rows_sel, oh_c, precision=lax.Precision.HIGHEST)
    bias = bias.reshape(rpb.shape[0], len(geos), nq, nk)
    bias = jnp.where(ok[None], bias, NEG_INF)
    h = rpb.shape[0]
    bias = bias.reshape(h // 2, 2, len(geos), nq, nk).transpose(0, 2, 1, 3, 4)
    return bias, deltas, geos


def _rope_tables(t):
    half = HEAD_DIM // 2
    freqs = ROPE_BASE ** (-jnp.arange(0, half, 2, dtype=F32) / half)
    pos = jnp.arange(t)
    prow, pcol = pos // GRID_W, pos % GRID_W
    ang_r = prow.astype(F32)[:, None] * freqs[None, :]
    ang_c = pcol.astype(F32)[:, None] * freqs[None, :]
    cos = jnp.concatenate([jnp.cos(ang_r)] * 2 + [jnp.cos(ang_c)] * 2, axis=-1)
    sin = jnp.concatenate([-jnp.sin(ang_r), jnp.sin(ang_r), -jnp.sin(ang_c), jnp.sin(ang_c)], axis=-1)
    return jnp.tile(cos, (1, 2)), jnp.tile(sin, (1, 2))


def _na_kernel(q_ref, k_ref, v_ref, cos_ref, sin_ref, swap_ref, bias_ref, geo_ref, o_ref,
               qr_s, qp_s, kr_s, v_s, *, lc, rows, kbh, with_ctx):
    w = GRID_W
    nq, nk = NA_QROWS * w, kbh * w
    scale = HEAD_DIM ** -0.5

    def rope(x):
        return x * cos_ref[...] + _dot_exact_rhs(x, swap_ref[...], 2) * sin_ref[...]

    assert np.log2(HEAD_DIM) % 2 == 0
    qr_s[...] = (rope(q_ref[0, lc:, :].astype(F32)) * scale).astype(BF16)
    qp_s[...] = (q_ref[0].astype(F32) * scale).astype(BF16)
    kr_s[...] = rope(k_ref[0, lc:, :].astype(F32)).astype(BF16)
    v_s[...] = v_ref[0].astype(BF16)
    k_ctx = k_ref[0, :lc, :].astype(BF16)

    lane_a = lax.broadcasted_iota(jnp.int32, (nq, PAIR_W), 1) < HEAD_DIM
    zero = jnp.zeros((), BF16)

    def by_head(q):
        return jnp.concatenate([jnp.where(lane_a, q, zero), jnp.where(lane_a, zero, q)], axis=0)

    def attend(chains):
        scores = [[_dot_nt(q, k) if b is None else _dot_nt(q, k) + b
                   for q, k, b, _ in parts] for parts in chains]
        mx = [functools.reduce(jnp.maximum, [jnp.max(s, axis=-1, keepdims=True) for s in sc])
              for sc in scores]
        es = [[jnp.exp(s - m) for s in sc] for sc, m in zip(scores, mx)]
        den = [functools.reduce(lambda a, b: a + b, [jnp.sum(e, axis=-1, keepdims=True) for e in ee])
               for ee in es]
        acc = [functools.reduce(lambda a, b: a + b,
                                [_dot(e.astype(BF16), part[3]) for e, part in zip(ee, parts)])
               for ee, parts in zip(es, chains)]
        return [a / d for a, d in zip(acc, den)]

    def blocks(it, carry):
        chains, q0s = [], []
        v_ctx = v_s[0:lc, :]
        for j in range(NA_BLOCKS_PER_ITER):
            blk = it * NA_BLOCKS_PER_ITER + j
            r0 = blk * NA_QROWS
            kr = jnp.clip(r0 - min(NA_KH, rows) // 2, 0, rows - kbh)
            geo = geo_ref[blk]
            q0 = pl.multiple_of(r0 * w, w)
            k0 = pl.multiple_of(kr * w, w)
            qb = qr_s[pl.ds(q0, nq), :]
            qpb = qp_s[pl.ds(lc + q0, nq), :]
            kb = kr_s[pl.ds(k0, nk), :]
            vb = v_s[pl.ds(lc + k0, nk), :]
            q0s.append(q0)
            chains.append([(by_head(qb), kb, bias_ref[0, geo].reshape(2 * nq, nk), vb),
                           (by_head(qpb), k_ctx, None, v_ctx)])
        outs = attend(chains)
        for q0, o in zip(q0s, outs):
            o_ref[0, pl.ds(lc + q0, nq), :] = jnp.where(lane_a, o[:nq], o[nq:]).astype(o_ref.dtype)
        return carry

    lax.fori_loop(0, rows // NA_QROWS // NA_BLOCKS_PER_ITER, blocks, 0)

    if with_ctx:
        v_ctx = v_s[0:lc, :]
        nblk = lc // nq
        outs = attend([[(by_head(qp_s[cblk * nq:(cblk + 1) * nq, :]), k_ctx, None, v_ctx)]
                       for cblk in range(nblk)])
        for cblk, o in enumerate(outs):
            o_ref[0, cblk * nq:(cblk + 1) * nq, :] = jnp.where(lane_a, o[:nq], o[nq:]).astype(o_ref.dtype)
    else:
        o_ref[0, 0:lc, :] = jnp.zeros((lc, PAIR_W), o_ref.dtype)


def _na(u_c, rpb, lc, with_ctx):
    b, r, fc = u_c.shape
    width = fc // 3
    npair = width // PAIR_W
    t = r - lc
    rows = t // GRID_W
    bias, deltas, geos = _na_bias_table(rpb, rows)
    kh, kbh, _, _ = _na_geometry(rows)
    geo_idx = jnp.asarray([geos.index(d) for d in deltas], jnp.int32)
    cos, sin = _rope_tables(t)
    quarter = HEAD_DIM // 4
    lane = np.arange(PAIR_W)
    src = np.where(lane % (2 * quarter) < quarter, lane + quarter, lane - quarter)
    swap = jnp.asarray((lane[:, None] == src[None, :]).astype(np.float32), BF16)
    nq, nk = NA_QROWS * GRID_W, kbh * GRID_W
    return pl.pallas_call(
        functools.partial(_na_kernel, lc=lc, rows=rows, kbh=kbh, with_ctx=with_ctx),
        grid=(npair, b),
        in_specs=[
            pl.BlockSpec((1, r, PAIR_W), lambda p, i: (i, 0, p)),
            pl.BlockSpec((1, r, PAIR_W), lambda p, i: (i, 0, npair + p)),
            pl.BlockSpec((1, r, PAIR_W), lambda p, i: (i, 0, 2 * npair + p)),
            pl.BlockSpec((t, PAIR_W), lambda p, i: (0, 0)),
            pl.BlockSpec((t, PAIR_W), lambda p, i: (0, 0)),
            pl.BlockSpec((PAIR_W, PAIR_W), lambda p, i: (0, 0)),
            pl.BlockSpec((1, len(geos), 2, nq, nk), lambda p, i: (p, 0, 0, 0, 0)),
            pl.BlockSpec(memory_space=pltpu.SMEM),
        ],
        out_specs=pl.BlockSpec((1, r, PAIR_W), lambda p, i: (i, 0, p)),
        out_shape=jax.ShapeDtypeStruct((b, r, width), ACT_DTYPE),
        scratch_shapes=[
            pltpu.VMEM((t, PAIR_W), BF16),
            pltpu.VMEM((r, PAIR_W), BF16),
            pltpu.VMEM((t, PAIR_W), BF16),
            pltpu.VMEM((r, PAIR_W), BF16),
        ],
        compiler_params=_cparams(("parallel", "parallel")),
        name="natten",
    )(u_c, u_c, u_c, cos, sin, swap, bias, geo_idx)


def _merge_kernel(x_ref, ug_ref, yf_ref, yb_ref, eb_ref, eg_ref, hf_ref, hb_ref, gb_ref, yc_ref,
                  mod_ref, lw_ref, lb_ref, e_ref, et_ref, wbr_ref, wo_ref, g_ref, o_ref):
    width = yf_ref.shape[2]

    def group_mean(z):
        s = _dot_exact_rhs(z, e_ref[...], 2)
        return _dot_exact_rhs(s, et_ref[...], 2) * (1.0 / HEAD_DIM)

    y = yf_ref[0] + yb_ref[0]
    dev = y - group_mean(y)
    yn = dev * lax.rsqrt(group_mean(dev * dev) + LNX_EPS)
    y_a = (yn * lw_ref[...] + lb_ref[...] + eb_ref[0]) * eg_ref[0]
    y_b = (hf_ref[0] + hb_ref[0]) * jax.nn.gelu(gb_ref[0])
    ug = ug_ref[0].astype(F32)
    m = (_sigmoid(ug[:, 0:width]) * _dot(y_a.astype(BF16), wbr_ref[0])
         + _sigmoid(ug[:, width:2 * width]) * _dot(y_b.astype(BF16), wbr_ref[1])
         + _sigmoid(ug[:, 2 * width:3 * width]) * _dot(yc_ref[0].astype(BF16), wbr_ref[2]))
    out = _dot(m.astype(BF16), wo_ref[...])
    o_ref[0] = x_ref[0] + mod_ref[0, 0, 2:3, :] * _rms(out, g_ref[...])


def _merge(xs, u_g, u_b, wkv_out, lru_out, y_c, mod, p, ng1, lc, skip_ctx):
    b, r, d = xs.shape
    width = y_c.shape[2]
    tm = ROW_TILE
    off = lc // tm if skip_ctx else 0
    nt = r // tm - off
    nctx = lc // tm
    heads = width // HEAD_DIM
    hid = np.arange(width) // HEAD_DIM
    e = (hid[:, None] == np.arange(V7X_LANES)[None, :]).astype(np.float32)
    e_bf, et_bf = jnp.asarray(e, BF16), jnp.asarray(e.T, BF16)
    assert heads <= V7X_LANES
    rowblk = lambda i, t: (i, t + off, 0)
    const2 = lambda i, t: (0, 0)
    act = lambda wd: pl.BlockSpec((1, tm, wd), rowblk)
    return pl.pallas_call(
        _merge_kernel,
        grid=(b, nt),
        in_specs=[
            act(d), act(3 * width), act(width), act(width), act(width), act(width), act(width), act(width),
            pl.BlockSpec((1, tm, width), lambda i, t: (i, t + off, 1)),
            act(width),
            pl.BlockSpec((1, 1, 6, d), lambda i, t: (i, jnp.where(t + off < nctx, 0, 1), 0, 0)),
            pl.BlockSpec((1, width), const2),
            pl.BlockSpec((1, width), const2),
            pl.BlockSpec((width, V7X_LANES), const2),
            pl.BlockSpec((V7X_LANES, width), const2),
            pl.BlockSpec((3, width, d), lambda i, t: (0, 0, 0)),
            pl.BlockSpec((d, d), const2),
            pl.BlockSpec((1, d), const2),
        ],
        out_specs=pl.BlockSpec((1, tm, d), lambda i, t: (i, t, 0)),
        out_shape=jax.ShapeDtypeStruct((b, nt * tm, d), F32),
        compiler_params=_cparams(("parallel", "parallel")),
        name="merge",
    )(xs, u_g, wkv_out[0], wkv_out[1], wkv_out[2], wkv_out[3], lru_out[0], lru_out[1], u_b, y_c,
      mod, p["rwkv_lnx_w"].reshape(1, -1), p["rwkv_lnx_b"].reshape(1, -1), e_bf, et_bf,
      p["w_branch"].astype(BF16), p["w_out"].astype(BF16), ng1)


def _ffn_kernel(x_ref, xp_ref, xn_ref, mod_ref, g2_ref, g3_ref, wi_ref, cw_ref, cb_ref, wo_ref, o_ref,
                *, tm, nctx, nt, dff):
    t = pl.program_id(1)
    first = jnp.logical_or(t == 0, t == nctx)
    last = jnp.logical_or(t == nctx - 1, t == nt - 1)
    nb = x_ref.shape[0]

    def hidden(xv, s):
        sh, sc = mod_ref[s, 0, 3:4, :], mod_ref[s, 0, 4:5, :]
        return (_rms(xv, g2_ref[...]) * (1.0 + sc) + sh).astype(BF16)

    h = jnp.concatenate([hidden(x_ref[s], s) for s in range(nb)], axis=0)
    h_halo = jnp.concatenate([hidden(ref[s], s) for s in range(nb) for ref in (xp_ref, xn_ref)], axis=0)
    gate = _dot(h, wi_ref[:, 0:dff])
    up = _dot(h, wi_ref[:, dff:2 * dff])
    g_halo = _dot(h_halo, wi_ref[:, 0:dff])
    rows = lax.broadcasted_iota(jnp.int32, (tm, 1), 0)
    acts = []
    for s in range(nb):
        g = gate[s * tm:(s + 1) * tm]
        g_prev = g_halo[(2 * s + 1) * FFN_HALO - 1:(2 * s + 1) * FFN_HALO, :]
        g_next = g_halo[(2 * s + 1) * FFN_HALO:(2 * s + 1) * FFN_HALO + 1, :]
        g_m1 = jnp.where(rows == 0, jnp.where(first, 0.0, g_prev), pltpu.roll(g, 1, 0))
        g_p1 = jnp.where(rows == tm - 1, jnp.where(last, 0.0, g_next), pltpu.roll(g, tm - 1, 0))
        gc = cw_ref[0:1, :] * g_m1 + cw_ref[1:2, :] * g + cw_ref[2:3, :] * g_p1 + cb_ref[...]
        acts.append((gc * _sigmoid(gc) * up[s * tm:(s + 1) * tm]).astype(BF16))
    out = _dot(jnp.concatenate(acts, axis=0), wo_ref[...])
    for s in range(nb):
        o_ref[s] = x_ref[s] + mod_ref[s, 0, 5:6, :] * _rms(out[s * tm:(s + 1) * tm], g3_ref[...])


def _ffn(x1, mod, p, ng2, ng3, lc):
    b, r, d = x1.shape
    tm = ROW_TILE
    nt = r // tm
    nctx = lc // tm
    hb = tm // FFN_HALO
    nhalo = r // FFN_HALO
    dff = p["ffn_w_out"].shape[0]
    nb = FFN_BATCHES if b % FFN_BATCHES == 0 else 1
    const2 = lambda i, t: (0, 0)
    resident = pl.Buffered(1)
    return pl.pallas_call(
        functools.partial(_ffn_kernel, tm=tm, nctx=nctx, nt=nt, dff=dff),
        grid=(b // nb, nt),
        in_specs=[
            pl.BlockSpec((nb, tm, d), lambda i, t: (i, t, 0)),
            pl.BlockSpec((nb, FFN_HALO, d), lambda i, t: (i, jnp.maximum(t * hb - 1, 0), 0)),
            pl.BlockSpec((nb, FFN_HALO, d), lambda i, t: (i, jnp.minimum((t + 1) * hb, nhalo - 1), 0)),
            pl.BlockSpec((nb, 1, 6, d), lambda i, t: (i, jnp.where(t < nctx, 0, 1), 0, 0)),
            pl.BlockSpec((1, d), const2),
            pl.BlockSpec((1, d), const2),
            pl.BlockSpec((d, 2 * dff), const2, pipeline_mode=resident),
            pl.BlockSpec((3, dff), const2),
            pl.BlockSpec((1, dff), const2),
            pl.BlockSpec((dff, d), const2, pipeline_mode=resident),
        ],
        out_specs=pl.BlockSpec((nb, tm, d), lambda i, t: (i, t, 0)),
        out_shape=jax.ShapeDtypeStruct((b, r, d), F32),
        compiler_params=_cparams(("parallel", "parallel")),
        name="conv_ffn",
    )(x1, x1, x1, mod, ng2, ng3, p["ffn_w_in"].astype(BF16), p["ffn_conv_w"],
      p["ffn_conv_b"].reshape(1, -1), p["ffn_w_out"].astype(BF16))


def _in_tile(r):
    for parts in (4, 8, 2, 1, 16, 32):
        if r % parts == 0 and (r // parts) % V7X_SUBLANES == 0 and r // parts <= 1152:
            return r // parts
    raise ValueError(f"no row tile for {r} rows")


def _col_tile(n):
    for parts in (1, 2, 3, 4, 6, 8):
        if n % parts == 0 and (n // parts) % V7X_LANES == 0 and n // parts <= 1792:
            return n // parts
    raise ValueError(f"no column tile for {n} columns")


def _layer(xs, mod, p, lc, update_ctx):
    width = p["rwkv_k_k"].shape[0]
    a_cols = 3 * width + W_LORA + A_LORA + G_LORA
    bounds = np.cumsum([0, a_cols, 2 * width, 3 * width, 3 * width])
    ng = p["norm_g"]
    w_in = p["w_in"].astype(BF16)
    tm = _in_tile(xs.shape[1])
    w_a, w_b, w_c, w_g = [w_in[:, lo:hi] for lo, hi in zip(bounds[:-1], bounds[1:])]
    mu_p, mu_n = p["rwkv_mu"][0], p["rwkv_mu"][1]
    shift_taps = jnp.stack([mu_p, 1.0 - mu_p - mu_n, mu_n, jnp.zeros_like(mu_p)])
    u_a = _in_proj_taps(xs, mod, ng[0:1], w_a, shift_taps, (-1, 0, 1), lc, tm, _col_tile(a_cols))
    ident = jnp.zeros((5, width), F32).at[2].set(1.0)
    conv_taps = jnp.concatenate(
        [jnp.concatenate([p["lru_conv_w"], p["lru_conv_b"][None]], axis=0), ident], axis=1)
    u_b = _in_proj_taps(xs, mod, ng[0:1], w_b, conv_taps, (-2, -1, 0, 1), lc, tm, _col_tile(2 * width))
    u_c = _in_proj(xs, mod, ng[0:1], w_c, lc, tm, _col_tile(3 * width))
    u_g = _in_proj(xs, mod, ng[0:1], w_g, lc, tm, _col_tile(3 * width))
    wkv_out, lru_out = _scans(u_a, u_b, p, lc)
    y_c = _na(u_c, p["na_rpb"], lc, update_ctx)
    x1 = _merge(xs, u_g, u_b, wkv_out, lru_out, y_c, mod, p, ng[1:2], lc, not update_ctx)
    return _ffn(x1, mod, p, ng[2:3], ng[3:4], lc if update_ctx else 0)


_PARAM_NAMES = ("norm_g", "w_in", "rwkv_mu", "rwkv_w0", "rwkv_w_up", "rwkv_a0", "rwkv_a_up", "rwkv_g_up",
                "rwkv_k_k", "rwkv_k_a", "rwkv_r_k", "rwkv_lnx_w", "rwkv_lnx_b", "lru_conv_w", "lru_conv_b",
                "lru_gate_a_w", "lru_gate_a_b", "lru_gate_x_w", "lru_gate_x_b", "lru_lambda", "na_rpb",
                "w_branch", "w_out", "ffn_w_in", "ffn_conv_w", "ffn_conv_b", "ffn_w_out")


def kernel(x, c, ctx, c_ctx, ada_w, ada_b, norm_g, w_in, rwkv_mu, rwkv_w0, rwkv_w_up, rwkv_a0, rwkv_a_up, rwkv_g_up, rwkv_k_k, rwkv_k_a, rwkv_r_k, rwkv_lnx_w, rwkv_lnx_b, lru_conv_w, lru_conv_b, lru_gate_a_w, lru_gate_a_b, lru_gate_x_w, lru_gate_x_b, lru_lambda, na_rpb, w_branch, w_out, ffn_w_in, ffn_conv_w, ffn_conv_b, ffn_w_out):
    stacked = dict(zip(_PARAM_NAMES, (norm_g, w_in, rwkv_mu, rwkv_w0, rwkv_w_up, rwkv_a0, rwkv_a_up,
                                      rwkv_g_up, rwkv_k_k, rwkv_k_a, rwkv_r_k, rwkv_lnx_w, rwkv_lnx_b,
                                      lru_conv_w, lru_conv_b, lru_gate_a_w, lru_gate_a_b, lru_gate_x_w,
                                      lru_gate_x_b, lru_lambda, na_rpb, w_branch, w_out, ffn_w_in,
                                      ffn_conv_w, ffn_conv_b, ffn_w_out)))
    depth = ada_w.shape[0]
    b, _, d = x.shape
    lc = ctx.shape[1]
    pad = (-(b + 1)) % V7X_SUBLANES
    cc = jnp.concatenate([c, c_ctx[None, :], jnp.zeros((pad, d), c.dtype)], axis=0)
    mod_all = _modulation(cc, ada_w, ada_b)
    xs = jnp.concatenate([ctx, x], axis=1)
    for l in range(depth):
        p = {name: val[l] for name, val in stacked.items()}
        p["rwkv_r_k"] = p["rwkv_r_k"].reshape(-1)
        m = mod_all[l].reshape(-1, 6, d)
        mod = jnp.stack([jnp.broadcast_to(m[b], (b, 6, d)), m[:b]], axis=1)
        xs = _layer(xs, mod, p, lc, l < depth - 1)
    return xs
```

```python
import functools

import numpy as np
import jax
import jax.numpy as jnp
from jax import lax
from jax.experimental import pallas as pl
from jax.experimental.pallas import tpu as pltpu

F32 = jnp.float32
BF16 = jnp.bfloat16
ACT_DTYPE = BF16

GRID_W = 64
HEAD_DIM = 64
W_LORA = 64
A_LORA = 64
G_LORA = 128
LNX_EPS = 64e-5
LRU_C = 8.0
NA_KH = 8
NA_KW = 16
NA_QROWS = 2
NA_BLOCKS_PER_ITER = 4
NA_BAND_ROWS = 10
ROPE_BASE = 10000.0
EPS = 1e-6
NEG_INF = -1e30

V7X_LANES = 128
V7X_SUBLANES = 8
V7X_VMEM_LIMIT_BYTES = 56 * 1024 * 1024

PAIR_W = 2 * HEAD_DIM
SCAN_CHUNK = 64
WKV_DECAY_SCALE = float(np.exp(-0.5))
ROW_TILE = 256
LRU_GROUP = 256
IN_HALO = 16
FFN_BATCHES = 4
FFN_HALO = 16


def _cparams(sem):
    return pltpu.CompilerParams(dimension_semantics=sem, vmem_limit_bytes=V7X_VMEM_LIMIT_BYTES)


def _split_bf16(x, parts):
    out = []
    rem = x
    for _ in range(parts - 1):
        hi = rem.astype(BF16)
        out.append(hi)
        rem = rem - hi.astype(F32)
    out.append(rem.astype(BF16))
    return out


def _dot(a, b):
    return jnp.dot(a, b, preferred_element_type=F32)


def _dot_nt(a, b):
    return lax.dot_general(a, b, (((1,), (1,)), ((), ())), preferred_element_type=F32)


def _dot_tn(a, b):
    return lax.dot_general(a, b, (((0,), (0,)), ((), ())), preferred_element_type=F32)


def _dot_exact_lhs(a_bf16, x, parts):
    acc = None
    for piece in _split_bf16(x, parts):
        t = _dot(a_bf16, piece)
        acc = t if acc is None else acc + t
    return acc


def _dot_exact_rhs(x, b_bf16, parts):
    acc = None
    for piece in _split_bf16(x, parts):
        t = _dot(piece, b_bf16)
        acc = t if acc is None else acc + t
    return acc


def _sigmoid(x):
    return 1.0 / (1.0 + jnp.exp(-x))


def _softplus(x):
    return jnp.maximum(x, 0.0) + jnp.log(1.0 + jnp.exp(-jnp.abs(x)))


def _rms(x, g):
    return x * lax.rsqrt(jnp.mean(x * x, axis=-1, keepdims=True) + EPS) * g


def _mod_kernel(c_ref, w_ref, b_ref, o_ref):
    c = c_ref[...]
    s = c * _sigmoid(c)
    o_ref[0] = _dot(s.astype(BF16), w_ref[0].astype(BF16)) + b_ref[0]


def _modulation(cc, ada_w, ada_b):
    depth, d, n = ada_w.shape
    rows = cc.shape[0]
    tn = n // 4
    return pl.pallas_call(
        _mod_kernel,
        grid=(depth, n // tn),
        in_specs=[
            pl.BlockSpec((rows, d), lambda l, j: (0, 0)),
            pl.BlockSpec((1, d, tn), lambda l, j: (l, 0, j)),
            pl.BlockSpec((1, 1, tn), lambda l, j: (l, 0, j)),
        ],
        out_specs=pl.BlockSpec((1, rows, tn), lambda l, j: (l, 0, j)),
        out_shape=jax.ShapeDtypeStruct((depth, rows, n), F32),
        compiler_params=_cparams(("parallel", "parallel")),
        name="modulation",
    )(cc, ada_w, ada_b.reshape(depth, 1, n))


def _in_kernel(x_ref, mod_ref, g_ref, w_ref, o_ref, hb_ref, *, lc, tm):
    @pl.when(pl.program_id(2) == 0)
    def _():
        y = _rms(x_ref[0], g_ref[...])
        row = pl.program_id(1) * tm + lax.broadcasted_iota(jnp.int32, (tm, 1), 0)
        is_ctx = row < lc
        sh = jnp.where(is_ctx, mod_ref[0, 0, 0:1, :], mod_ref[0, 1, 0:1, :])
        sc = jnp.where(is_ctx, mod_ref[0, 0, 1:2, :], mod_ref[0, 1, 1:2, :])
        hb_ref[...] = (y * (1.0 + sc) + sh).astype(BF16)

    o_ref[0] = _dot(hb_ref[...], w_ref[...]).astype(o_ref.dtype)


def _in_taps_kernel(x_ref, xp_ref, xn_ref, mod_ref, g_ref, w_ref, taps_ref, o_ref, hb_ref,
                    *, lc, r, tm, shifts):
    t = pl.program_id(1)
    halo = IN_HALO
    ns = V7X_SUBLANES

    @pl.when(pl.program_id(2) == 0)
    def _():
        def hidden(xv, row0):
            row = row0 + lax.broadcasted_iota(jnp.int32, (xv.shape[0], 1), 0)
            is_ctx = row < lc
            sh = jnp.where(is_ctx, mod_ref[0, 0, 0:1, :], mod_ref[0, 1, 0:1, :])
            sc = jnp.where(is_ctx, mod_ref[0, 0, 1:2, :], mod_ref[0, 1, 1:2, :])
            return (_rms(xv, g_ref[...]) * (1.0 + sc) + sh).astype(BF16)

        hb_ref[0:halo, :] = hidden(xp_ref[0], t * tm - halo)
        hb_ref[halo:halo + tm, :] = hidden(x_ref[0], t * tm)
        hb_ref[halo + tm:2 * halo + tm, :] = hidden(xn_ref[0], (t + 1) * tm)

    tn = o_ref.shape[2]
    lead = halo // ns
    sub_row = lax.broadcasted_iota(jnp.int32, (1, ns, 1), 1)
    tiles = tm // ns
    u = _dot(hb_ref[...], w_ref[...]).reshape(tiles + 2 * lead, ns, tn)
    row = (t * tm + ns * lax.broadcasted_iota(jnp.int32, (tiles, 1, 1), 0)
           + lax.broadcasted_iota(jnp.int32, (1, ns, 1), 1))
    from_start = row - jnp.where(row < lc, 0, lc)
    to_end = jnp.where(row < lc, lc, r) - 1 - row
    acc = None
    for j, s in enumerate(shifts):
        coef = taps_ref[j:j + 1, :].reshape(1, 1, tn)
        if s == 0:
            term = u[lead:lead + tiles]
        elif s < 0:
            rolled = pltpu.roll(u[lead - 1:lead + tiles], -s, 1)
            term = jnp.where(sub_row >= -s, rolled[1:], rolled[:-1])
            term = jnp.where(from_start >= -s, term, 0.0)
        else:
            rolled = pltpu.roll(u[lead:lead + tiles + 1], ns - s, 1)
            term = jnp.where(sub_row < ns - s, rolled[:-1], rolled[1:])
            term = jnp.where(to_end >= s, term, 0.0)
        acc = coef * term if acc is None else acc + coef * term
    acc = acc + taps_ref[len(shifts):len(shifts) + 1, :].reshape(1, 1, tn)
    o_ref[0] = acc.reshape(tm, tn).astype(o_ref.dtype)


def _in_proj_taps(xs, mod, g, w, taps, shifts, lc, tm, tn):
    b, r, d = xs.shape
    n = w.shape[1]
    hb = tm // IN_HALO
    nhalo = r // IN_HALO
    return pl.pallas_call(
        functools.partial(_in_taps_kernel, lc=lc, r=r, tm=tm, shifts=shifts),
        grid=(b, r // tm, n // tn),
        in_specs=[
            pl.BlockSpec((1, tm, d), lambda i, t, j: (i, t, 0)),
            pl.BlockSpec((1, IN_HALO, d), lambda i, t, j: (i, jnp.maximum(t * hb - 1, 0), 0)),
            pl.BlockSpec((1, IN_HALO, d), lambda i, t, j: (i, jnp.minimum((t + 1) * hb, nhalo - 1), 0)),
            pl.BlockSpec((1, 2, 6, d), lambda i, t, j: (i, 0, 0, 0)),
            pl.BlockSpec((1, d), lambda i, t, j: (0, 0)),
            pl.BlockSpec((d, tn), lambda i, t, j: (0, j)),
            pl.BlockSpec((len(shifts) + 1, tn), lambda i, t, j: (0, j)),
        ],
        out_specs=pl.BlockSpec((1, tm, tn), lambda i, t, j: (i, t, j)),
        out_shape=jax.ShapeDtypeStruct((b, r, n), F32),
        scratch_shapes=[pltpu.VMEM((tm + 2 * IN_HALO, d), BF16)],
        compiler_params=_cparams(("parallel", "parallel", "arbitrary")),
        name="in_proj_taps",
    )(xs, xs, xs, mod, g, w, taps)


def _in_proj(xs, mod, g, w, lc, tm, tn):
    b, r, d = xs.shape
    n = w.shape[1]
    return pl.pallas_call(
        functools.partial(_in_kernel, lc=lc, tm=tm),
        grid=(b, r // tm, n // tn),
        in_specs=[
            pl.BlockSpec((1, tm, d), lambda i, t, j: (i, t, 0)),
            pl.BlockSpec((1, 2, 6, d), lambda i, t, j: (i, 0, 0, 0)),
            pl.BlockSpec((1, d), lambda i, t, j: (0, 0)),
            pl.BlockSpec((d, tn), lambda i, t, j: (0, j)),
        ],
        out_specs=pl.BlockSpec((1, tm, tn), lambda i, t, j: (i, t, j)),
        out_shape=jax.ShapeDtypeStruct((b, r, n), ACT_DTYPE),
        scratch_shapes=[pltpu.VMEM((tm, d), BF16)],
        compiler_params=_cparams(("parallel", "parallel", "arbitrary")),
        name="in_proj",
    )(xs, mod, g, w)


def _stack_pair(x, lane_a):
    return jnp.concatenate([jnp.where(lane_a, x, 0.0), jnp.where(lane_a, 0.0, x)], axis=0)


def _wkv_chains(chains, c):
    n = 2 * c
    lane_a = lax.broadcasted_iota(jnp.int32, (c, PAIR_W), 1) < HEAD_DIM
    stack = lambda x: _stack_pair(x.astype(BF16), lane_a)
    nch = len(chains)

    ops = []
    for ch in chains:
        r, kd, v, kkn, a, lw, cum = ch[:7]
        tot = jnp.sum(lw, axis=0, keepdims=True)
        e_inv = jnp.exp(-cum)
        e_rem = jnp.exp(tot - cum)
        akk = a * kkn
        lhs = jnp.concatenate([stack(-kkn * jnp.exp(cum - lw)), stack(r * jnp.exp(cum))], axis=0)
        bk = jnp.concatenate([stack(akk * e_inv), stack(kd * e_inv)], axis=0)
        bkc = jnp.concatenate([stack(akk * e_rem), stack(kd * e_rem)], axis=0)
        ops.append((lhs, bk, bkc, stack(v), tot))

    g = [_dot_nt(o[0], o[1]) for o in ops]
    ls = [_dot_nt(o[0], ch[7].astype(BF16)) for o, ch in zip(ops, chains)]
    m_b = [jnp.where(ch[8], gi[:n, :n], 0.0) for ch, gi in zip(chains, g)]
    m_k = [jnp.where(ch[8], gi[:n, n:], 0.0).astype(BF16) for ch, gi in zip(chains, g)]
    n_bk = [jnp.concatenate([jnp.where(ch[9], gi[n:, :n], 0.0).astype(BF16),
                             jnp.where(ch[9], gi[n:, n:], 0.0).astype(BF16)], axis=1)
            for ch, gi in zip(chains, g)]
    mkv = [_dot(mk, o[3]) for mk, o in zip(m_k, ops)]

    eye = (lax.broadcasted_iota(jnp.int32, (n, n), 0) == lax.broadcasted_iota(jnp.int32, (n, n), 1))
    t_inv = [jnp.where(eye, 1.0, 0.0) + m for m in m_b]
    m_pow = [m.astype(BF16) for m in m_b]
    span = 1
    while 2 * span < c:
        m_pow = [_dot(mp, mp).astype(BF16) for mp in m_pow]
        t_inv = [t + _dot(t.astype(BF16), mp) for t, mp in zip(t_inv, m_pow)]
        span *= 2

    u = [_dot(t.astype(BF16), (l[:n] + mv).astype(BF16)).astype(BF16)
         for t, l, mv in zip(t_inv, ls, mkv)]
    uv = [jnp.concatenate([ui, o[3]], axis=0) for ui, o in zip(u, ops)]
    ys = [l[n:] + _dot(nb, x) for l, nb, x in zip(ls, n_bk, uv)]
    s_new = [ch[7] * jnp.exp(o[4]) + _dot_tn(x, o[2]) for ch, o, x in zip(chains, ops, uv)]
    return [(ys[j][:c] + ys[j][c:], s_new[j]) for j in range(nch)]


def _scan_kernel(uf_ref, ub_ref, xf_ref, xb_ref,
                 w0_ref, wup_ref, a0_ref, aup_ref, gup_ref, kk_ref, ka_ref, rk_ref,
                 tri_ref, ms_ref, mi_ref, bd_ref,
                 wa_ref, ba_ref, wx_ref, bx_ref, lam_ref,
                 yf_ref, yb_ref, eb_ref, eg_ref, hf_ref, hb_ref, s_ref, h_ref, *, c):
    width = yf_ref.shape[2]
    npair = width // PAIR_W

    @pl.when(pl.program_id(1) == 0)
    def _():
        s_ref[...] = jnp.zeros(s_ref.shape, F32)
        h_ref[...] = jnp.zeros(h_ref.shape, F32)

    lru_gates = [_lru_gates(x_ref, wa_ref, wx_ref, d) for d, x_ref in enumerate((xf_ref, xb_ref))]

    us = (uf_ref[0], ub_ref[0])
    o = 3 * width
    w_lo = [jnp.tanh(u[:, o:o + W_LORA]).astype(BF16) for u in us]
    a_lo = [u[:, o + W_LORA:o + W_LORA + A_LORA].astype(BF16) for u in us]
    wpre = [w0_ref[d:d + 1, :] + _dot(w_lo[d], wup_ref[d]) for d in range(2)]
    apre = [a0_ref[d:d + 1, :] + _dot(a_lo[d], aup_ref[d]) for d in range(2)]
    apre_rev = a0_ref[1:2, :] + _dot(a_lo[0], aup_ref[1])
    g_lo = _sigmoid(us[0][:, o + W_LORA + A_LORA:o + W_LORA + A_LORA + G_LORA]).astype(BF16)
    eg_ref[0] = _dot(g_lo, gup_ref[...])

    bd = bd_ref[...]
    masks = [(ms_ref[d] > 0.5, mi_ref[d] > 0.5) for d in range(2)]
    def cols(d, part, p):
        return us[d][:, part * width + p * PAIR_W:part * width + (p + 1) * PAIR_W]

    lw, cum, kkraw, ss = {}, {}, {}, {}
    for p in range(npair):
        sl = slice(p * PAIR_W, (p + 1) * PAIR_W)
        for d in range(2):
            lw[d, p] = -WKV_DECAY_SCALE * _sigmoid(wpre[d][:, sl])
            cum[d, p] = _dot_exact_lhs(tri_ref[d], lw[d, p], 2)
            kkraw[d, p] = cols(d, 1, p) * kk_ref[:, sl]
            ss[d, p] = _dot((kkraw[d, p] * kkraw[d, p]).astype(BF16), bd)

    keys = [(d, p) for p in range(npair) for d in range(2)]
    chains, rkk = [], []
    for d, p in keys:
        sl = slice(p * PAIR_W, (p + 1) * PAIR_W)
        r, k, v = cols(d, 0, p), cols(d, 1, p), cols(d, 2, p)
        a = _sigmoid(apre[d][:, sl])
        kd = k * (1.0 + (a - 1.0) * ka_ref[:, sl])
        kkn = kkraw[d, p] * lax.rsqrt(jnp.maximum(ss[d, p], 1e-24))
        chains.append((r, kd, v, kkn, a, lw[d, p], cum[d, p], s_ref[d, p]) + masks[d])
        if d == 0:
            kd_rev = k * (1.0 + (_sigmoid(apre_rev[:, sl]) - 1.0) * ka_ref[:, sl])
            rkk.append(_split_bf16(r * (kd + kd_rev) * rk_ref[:, sl], 2))

    rk_hi = [_dot(s[0], bd) for s in rkk]
    rk_lo = [_dot(s[1], bd) for s in rkk]
    for (d, p), (y, s_new) in zip(keys, _wkv_chains(chains, c)):
        s_ref[d, p] = s_new
        (yf_ref, yb_ref)[d][0, :, p * PAIR_W:(p + 1) * PAIR_W] = y
    for p in range(npair):
        eb_ref[0, :, p * PAIR_W:(p + 1) * PAIR_W] = (rk_hi[p] + rk_lo[p]) * cols(0, 2, p)

    for d, (x_ref, out_ref) in enumerate(((xf_ref, hf_ref), (xb_ref, hb_ref))):
        _lru_step(x_ref, lru_gates[d], ba_ref[d:d + 1, :], bx_ref[d:d + 1, :], lam_ref, out_ref, h_ref, c, d)


def _wkv_consts(c):
    t = np.arange(c)
    lower = (t[:, None] >= t[None, :])
    tri = np.stack([lower, lower.T]).astype(np.float32)
    strict = np.stack([t[:, None] > t[None, :], t[:, None] < t[None, :]])
    incl = np.stack([lower, lower.T])
    ms = np.tile(strict, (1, 2, 2)).astype(np.float32)
    mi = np.tile(incl, (1, 2, 2)).astype(np.float32)
    h = np.arange(PAIR_W) // HEAD_DIM
    bd = (h[:, None] == h[None, :]).astype(np.float32)
    return jnp.asarray(tri, BF16), jnp.asarray(ms), jnp.asarray(mi), jnp.asarray(bd, BF16)


def _scans(u_a, u_b, p, lc):
    b, r, fa = u_a.shape
    width = p["rwkv_k_k"].shape[0]
    c = SCAN_CHUNK
    nc, nl = lc // c, (r - lc) // c
    nchunk = nc + nl
    npair = width // PAIR_W
    ng = width // LRU_GROUP
    tri, ms, mi, bd = _wkv_consts(c)
    const4 = lambda bi, i: (0, 0, 0, 0)

    def rev(i):
        return jnp.where(i < nc, nc - 1 - i, 2 * nc + nl - 1 - i)

    cur_f = lambda bi, i: (bi, i, 0)
    cur_b = lambda bi, i: (bi, rev(i), 0)
    const2 = lambda bi, i: (0, 0)
    const3 = lambda bi, i: (0, 0, 0)

    row = lambda x: x.reshape(1, -1)
    out_sd = jax.ShapeDtypeStruct((b, r, width), F32)
    outs = pl.pallas_call(
        functools.partial(_scan_kernel, c=c),
        grid=(b, nchunk),
        in_specs=[
            pl.BlockSpec((1, c, fa), cur_f),
            pl.BlockSpec((1, c, fa), cur_b),
            pl.BlockSpec((1, c, width), cur_f),
            pl.BlockSpec((1, c, width), cur_b),
            pl.BlockSpec((2, width), const2),
            pl.BlockSpec((2, W_LORA, width), const3),
            pl.BlockSpec((2, width), const2),
            pl.BlockSpec((2, A_LORA, width), const3),
            pl.BlockSpec((G_LORA, width), const2),
            pl.BlockSpec((1, width), const2),
            pl.BlockSpec((1, width), const2),
            pl.BlockSpec((1, width), const2),
            pl.BlockSpec((2, c, c), const3),
            pl.BlockSpec((2, 2 * c, 2 * c), const3),
            pl.BlockSpec((2, 2 * c, 2 * c), const3),
            pl.BlockSpec((PAIR_W, PAIR_W), const2),
            pl.BlockSpec((2, ng, LRU_GROUP, LRU_GROUP), const4),
            pl.BlockSpec((2, width), const2),
            pl.BlockSpec((2, ng, LRU_GROUP, LRU_GROUP), const4),
            pl.BlockSpec((2, width), const2),
            pl.BlockSpec((2, width), const2),
        ],
        out_specs=[
            pl.BlockSpec((1, c, width), cur_f),
            pl.BlockSpec((1, c, width), cur_b),
            pl.BlockSpec((1, c, width), cur_f),
            pl.BlockSpec((1, c, width), cur_f),
            pl.BlockSpec((1, c, width), cur_f),
            pl.BlockSpec((1, c, width), cur_b),
        ],
        out_shape=[out_sd] * 6,
        scratch_shapes=[
            pltpu.VMEM((2, npair, PAIR_W, PAIR_W), F32),
            pltpu.VMEM((2, width), F32),
        ],
        compiler_params=_cparams(("parallel", "arbitrary")),
        name="scans",
    )(u_a, u_a, u_b, u_b,
      p["rwkv_w0"], p["rwkv_w_up"].astype(BF16), p["rwkv_a0"],
      p["rwkv_a_up"].astype(BF16), p["rwkv_g_up"].astype(BF16),
      row(p["rwkv_k_k"]), row(p["rwkv_k_a"]), row(p["rwkv_r_k"]),
      tri, ms, mi, bd,
      _block_diag_groups(p["lru_gate_a_w"]), p["lru_gate_a_b"],
      _block_diag_groups(p["lru_gate_x_w"]), p["lru_gate_x_b"], p["lru_lambda"])
    return outs[:4], outs[4:]


def _lru_scan(a, bv, h0, out_ref, c, reverse):
    ns = V7X_SUBLANES
    tiles = c // ns
    width = a.shape[1]
    a = a.reshape(tiles, ns, width)
    bv = bv.reshape(tiles, ns, width)
    sub_row = lax.broadcasted_iota(jnp.int32, (1, ns, 1), 1)
    s = 1
    while s < ns:
        keep = sub_row < ns - s if reverse else sub_row >= s
        shift = ns - s if reverse else s
        a_sh = jnp.where(keep, pltpu.roll(a, shift, 1), 1.0)
        b_sh = jnp.where(keep, pltpu.roll(bv, shift, 1), 0.0)
        bv = a * b_sh + bv
        a = a * a_sh
        s *= 2
    state = h0
    for k in (range(tiles - 1, -1, -1) if reverse else range(tiles)):
        hk = bv[k] + a[k] * state
        out_ref[0, k * ns:(k + 1) * ns, :] = hk
        state = hk[0:1, :] if reverse else hk[ns - 1:ns, :]
    return state


def _lru_gates(u_ref, wa_ref, wx_ref, d):
    xcb = u_ref[0].astype(BF16)
    ng = xcb.shape[1] // LRU_GROUP
    ga = jnp.concatenate([_dot(xcb[:, g * LRU_GROUP:(g + 1) * LRU_GROUP], wa_ref[d, g])
                          for g in range(ng)], axis=1)
    gx = jnp.concatenate([_dot(xcb[:, g * LRU_GROUP:(g + 1) * LRU_GROUP], wx_ref[d, g])
                          for g in range(ng)], axis=1)
    return ga, gx


def _lru_step(u_ref, gates, bias_a, bias_x, lam_ref, out_ref, h_ref, c, d):
    ga, gx = gates
    rg = _sigmoid(ga + bias_a)
    ig = _sigmoid(gx + bias_x)
    a = jnp.exp(-LRU_C * rg * _softplus(-lam_ref[d:d + 1, :]))
    bv = jnp.sqrt(1.0 - a * a) * (ig * u_ref[0])
    h_ref[d:d + 1, :] = _lru_scan(a, bv, h_ref[d:d + 1, :], out_ref, c, d == 1)


def _block_diag_groups(w):
    nd, nb, bw, _ = w.shape
    per = LRU_GROUP // bw
    w = w.reshape(nd, nb // per, per, bw, bw)
    eye = jnp.eye(per, dtype=w.dtype)
    out = jnp.einsum("dgpij,pq->dgpiqj", w, eye)
    return out.reshape(nd, nb // per, LRU_GROUP, LRU_GROUP).astype(BF16)


def _na_geometry(rows):
    assert rows >= NA_BAND_ROWS
    kh = min(NA_KH, rows)
    kbh = NA_BAND_ROWS
    deltas = []
    for blk in range(rows // NA_QROWS):
        r0 = blk * NA_QROWS
        kr = int(np.clip(r0 - kh // 2, 0, rows - kbh))
        deltas.append(r0 - kr)
    return kh, kbh, deltas, sorted(set(deltas))


def _na_bias_table(rpb, rows):
    w = GRID_W
    kh, kbh, deltas, geos = _na_geometry(rows)
    nq, nk = NA_QROWS * w, kbh * w
    drow = np.zeros((len(geos), nq, nk), np.int32)
    dcol = np.zeros((len(geos), nq, nk), np.int32)
    ok = np.zeros((len(geos), nq, nk), bool)
    qc = np.tile(np.arange(w), NA_QROWS)
    qr = np.repeat(np.arange(NA_QROWS), w)
    kc = np.tile(np.arange(w), kbh)
    kj = np.repeat(np.arange(kbh), w)
    cs = np.clip(qc - NA_KW // 2, 0, w - NA_KW)
    col_ok = (kc[None, :] >= cs[:, None]) & (kc[None, :] < cs[:, None] + NA_KW)
    dc = np.clip(kc[None, :] - qc[:, None] + NA_KW - 1, 0, 2 * NA_KW - 2)
    for gi, delta in enumerate(geos):
        seen = None
        for blk, dl in enumerate(deltas):
            if dl != delta:
                continue
            r0 = blk * NA_QROWS
            kr = r0 - delta
            qrow = r0 + qr
            krow = kr + kj
            rs = np.clip(qrow - kh // 2, 0, rows - kh)
            row_ok = (krow[None, :] >= rs[:, None]) & (krow[None, :] < rs[:, None] + kh)
            dr = np.clip(krow[None, :] - qrow[:, None] + NA_KH - 1, 0, 2 * NA_KH - 2)
            cur = (row_ok & col_ok, dr)
            if seen is None:
                seen = cur
            else:
                assert np.array_equal(seen[0], cur[0]) and np.array_equal(seen[1], cur[1])
        ok[gi], drow[gi], dcol[gi] = seen[0], seen[1], dc
    dr5 = drow.reshape(len(geos), NA_QROWS, w, kbh, w)[:, :, 0, :, 0]
    dc4 = dc.reshape(NA_QROWS, w, kbh, w)[0, :, 0, :]
    oh_r = (dr5[..., None] == np.arange(rpb.shape[1])).astype(np.float32)
    oh_c = (dc4[None] == np.arange(rpb.shape[2])[:, None, None]).astype(np.float32)
    h = rpb.shape[0]
    rpb2 = rpb.reshape(h // 2, 2, rpb.shape[1], rpb.shape[2])
    rows_sel = jnp.einsum("gqjr,phrc->pghqjc", oh_r, rpb2, precision=lax.Precision.HIGHEST)
    bias = jnp.einsum("pghqjc,cxy->pghqxjy", rows_sel, oh_c, precision=lax.Precision.HIGHEST)
    bias = bias.reshape(h // 2, len(geos), 2, nq, nk)
    bias = jnp.where(ok[None, :, None], bias, NEG_INF)
    return bias, deltas, geos


def _rope_tables(t):
    half = HEAD_DIM // 2
    freqs = ROPE_BASE ** (-jnp.arange(0, half, 2, dtype=F32) / half)
    pos = jnp.arange(t)
    prow, pcol = pos // GRID_W, pos % GRID_W
    ang_r = prow.astype(F32)[:, None] * freqs[None, :]
    ang_c = pcol.astype(F32)[:, None] * freqs[None, :]
    cos = jnp.concatenate([jnp.cos(ang_r)] * 2 + [jnp.cos(ang_c)] * 2, axis=-1)
    sin = jnp.concatenate([-jnp.sin(ang_r), jnp.sin(ang_r), -jnp.sin(ang_c), jnp.sin(ang_c)], axis=-1)
    return jnp.tile(cos, (1, 2)), jnp.tile(sin, (1, 2))


def _na_kernel(q_ref, k_ref, v_ref, cos_ref, sin_ref, swap_ref, bias_ref, geo_ref, o_ref,
               qr_s, qp_s, kr_s, v_s, *, lc, rows, kbh, with_ctx):
    w = GRID_W
    nq, nk = NA_QROWS * w, kbh * w
    scale = HEAD_DIM ** -0.5

    def rope(x):
        return x * cos_ref[...] + _dot_exact_rhs(x, swap_ref[...], 2) * sin_ref[...]

    assert np.log2(HEAD_DIM) % 2 == 0
    qr_s[...] = (rope(q_ref[0, lc:, :].astype(F32)) * scale).astype(BF16)
    qp_s[...] = (q_ref[0].astype(F32) * scale).astype(BF16)
    kr_s[...] = rope(k_ref[0, lc:, :].astype(F32)).astype(BF16)
    v_s[...] = v_ref[0].astype(BF16)
    k_ctx = k_ref[0, :lc, :].astype(BF16)

    lane_a = lax.broadcasted_iota(jnp.int32, (nq, PAIR_W), 1) < HEAD_DIM
    zero = jnp.zeros((), BF16)

    def by_head(q):
        return jnp.concatenate([jnp.where(lane_a, q, zero), jnp.where(lane_a, zero, q)], axis=0)

    def attend(chains):
        scores = [[_dot_nt(q, k) if b is None else _dot_nt(q, k) + b
                   for q, k, b, _ in parts] for parts in chains]
        mx = [functools.reduce(jnp.maximum, [jnp.max(s, axis=-1, keepdims=True) for s in sc])
              for sc in scores]
        es = [[jnp.exp(s - m) for s in sc] for sc, m in zip(scores, mx)]
        den = [functools.reduce(lambda a, b: a + b, [jnp.sum(e, axis=-1, keepdims=True) for e in ee])
               for ee in es]
        acc = [functools.reduce(lambda a, b: a + b,
                                [_dot(e.astype(BF16), part[3]) for e, part in zip(ee, parts)])
               for ee, parts in zip(es, chains)]
        return [a / d for a, d in zip(acc, den)]

    def blocks(it, carry):
        chains, q0s = [], []
        v_ctx = v_s[0:lc, :]
        for j in range(NA_BLOCKS_PER_ITER):
            blk = it * NA_BLOCKS_PER_ITER + j
            r0 = blk * NA_QROWS
            kr = jnp.clip(r0 - min(NA_KH, rows) // 2, 0, rows - kbh)
            geo = geo_ref[blk]
            q0 = pl.multiple_of(r0 * w, w)
            k0 = pl.multiple_of(kr * w, w)
            qb = qr_s[pl.ds(q0, nq), :]
            qpb = qp_s[pl.ds(lc + q0, nq), :]
            kb = kr_s[pl.ds(k0, nk), :]
            vb = v_s[pl.ds(lc + k0, nk), :]
            q0s.append(q0)
            chains.append([(by_head(qb), kb, bias_ref[0, geo].reshape(2 * nq, nk), vb),
                           (by_head(qpb), k_ctx, None, v_ctx)])
        outs = attend(chains)
        for q0, o in zip(q0s, outs):
            o_ref[0, pl.ds(lc + q0, nq), :] = jnp.where(lane_a, o[:nq], o[nq:]).astype(o_ref.dtype)
        return carry

    lax.fori_loop(0, rows // NA_QROWS // NA_BLOCKS_PER_ITER, blocks, 0)

    if with_ctx:
        v_ctx = v_s[0:lc, :]
        nblk = lc // nq
        outs = attend([[(by_head(qp_s[cblk * nq:(cblk + 1) * nq, :]), k_ctx, None, v_ctx)]
                       for cblk in range(nblk)])
        for cblk, o in enumerate(outs):
            o_ref[0, cblk * nq:(cblk + 1) * nq, :] = jnp.where(lane_a, o[:nq], o[nq:]).astype(o_ref.dtype)
    else:
        o_ref[0, 0:lc, :] = jnp.zeros((lc, PAIR_W), o_ref.dtype)


def _na(u_c, rpb, lc, with_ctx):
    b, r, fc = u_c.shape
    width = fc // 3
    npair = width // PAIR_W
    t = r - lc
    rows = t // GRID_W
    bias, deltas, geos = _na_bias_table(rpb, rows)
    kh, kbh, _, _ = _na_geometry(rows)
    geo_idx = jnp.asarray([geos.index(d) for d in deltas], jnp.int32)
    cos, sin = _rope_tables(t)
    quarter = HEAD_DIM // 4
    lane = np.arange(PAIR_W)
    src = np.where(lane % (2 * quarter) < quarter, lane + quarter, lane - quarter)
    swap = jnp.asarray((lane[:, None] == src[None, :]).astype(np.float32), BF16)
    nq, nk = NA_QROWS * GRID_W, kbh * GRID_W
    return pl.pallas_call(
        functools.partial(_na_kernel, lc=lc, rows=rows, kbh=kbh, with_ctx=with_ctx),
        grid=(npair, b),
        in_specs=[
            pl.BlockSpec((1, r, PAIR_W), lambda p, i: (i, 0, p)),
            pl.BlockSpec((1, r, PAIR_W), lambda p, i: (i, 0, npair + p)),
            pl.BlockSpec((1, r, PAIR_W), lambda p, i: (i, 0, 2 * npair + p)),
            pl.BlockSpec((t, PAIR_W), lambda p, i: (0, 0)),
            pl.BlockSpec((t, PAIR_W), lambda p, i: (0, 0)),
            pl.BlockSpec((PAIR_W, PAIR_W), lambda p, i: (0, 0)),
            pl.BlockSpec((1, len(geos), 2, nq, nk), lambda p, i: (p, 0, 0, 0, 0)),
            pl.BlockSpec(memory_space=pltpu.SMEM),
        ],
        out_specs=pl.BlockSpec((1, r, PAIR_W), lambda p, i: (i, 0, p)),
        out_shape=jax.ShapeDtypeStruct((b, r, width), ACT_DTYPE),
        scratch_shapes=[
            pltpu.VMEM((t, PAIR_W), BF16),
            pltpu.VMEM((r, PAIR_W), BF16),
            pltpu.VMEM((t, PAIR_W), BF16),
            pltpu.VMEM((r, PAIR_W), BF16),
        ],
        compiler_params=_cparams(("parallel", "parallel")),
        name="natten",
    )(u_c, u_c, u_c, cos, sin, swap, bias, geo_idx)


def _merge_kernel(x_ref, ug_ref, yf_ref, yb_ref, eb_ref, eg_ref, hf_ref, hb_ref, gb_ref, yc_ref,
                  mod_ref, lw_ref, lb_ref, e_ref, et_ref, wbr_ref, wo_ref, g_ref, o_ref):
    width = yf_ref.shape[2]

    def group_mean(z):
        s = _dot_exact_rhs(z, e_ref[...], 2)
        return _dot_exact_rhs(s, et_ref[...], 2) * (1.0 / HEAD_DIM)

    y = yf_ref[0] + yb_ref[0]
    dev = y - group_mean(y)
    yn = dev * lax.rsqrt(group_mean(dev * dev) + LNX_EPS)
    y_a = (yn * lw_ref[...] + lb_ref[...] + eb_ref[0]) * eg_ref[0]
    y_b = (hf_ref[0] + hb_ref[0]) * jax.nn.gelu(gb_ref[0])
    ug = ug_ref[0].astype(F32)
    m = (_sigmoid(ug[:, 0:width]) * _dot(y_a.astype(BF16), wbr_ref[0])
         + _sigmoid(ug[:, width:2 * width]) * _dot(y_b.astype(BF16), wbr_ref[1])
         + _sigmoid(ug[:, 2 * width:3 * width]) * _dot(yc_ref[0].astype(BF16), wbr_ref[2]))
    out = _dot(m.astype(BF16), wo_ref[...])
    o_ref[0] = x_ref[0] + mod_ref[0, 0, 2:3, :] * _rms(out, g_ref[...])


def _merge(xs, u_g, u_b, wkv_out, lru_out, y_c, mod, p, ng1, lc, skip_ctx):
    b, r, d = xs.shape
    width = y_c.shape[2]
    tm = ROW_TILE
    off = lc // tm if skip_ctx else 0
    nt = r // tm - off
    nctx = lc // tm
    heads = width // HEAD_DIM
    hid = np.arange(width) // HEAD_DIM
    e = (hid[:, None] == np.arange(V7X_LANES)[None, :]).astype(np.float32)
    e_bf, et_bf = jnp.asarray(e, BF16), jnp.asarray(e.T, BF16)
    assert heads <= V7X_LANES
    rowblk = lambda i, t: (i, t + off, 0)
    const2 = lambda i, t: (0, 0)
    act = lambda wd: pl.BlockSpec((1, tm, wd), rowblk)
    return pl.pallas_call(
        _merge_kernel,
        grid=(b, nt),
        in_specs=[
            act(d), act(3 * width), act(width), act(width), act(width), act(width), act(width), act(width),
            pl.BlockSpec((1, tm, width), lambda i, t: (i, t + off, 1)),
            act(width),
            pl.BlockSpec((1, 1, 6, d), lambda i, t: (i, jnp.where(t + off < nctx, 0, 1), 0, 0)),
            pl.BlockSpec((1, width), const2),
            pl.BlockSpec((1, width), const2),
            pl.BlockSpec((width, V7X_LANES), const2),
            pl.BlockSpec((V7X_LANES, width), const2),
            pl.BlockSpec((3, width, d), lambda i, t: (0, 0, 0)),
            pl.BlockSpec((d, d), const2),
            pl.BlockSpec((1, d), const2),
        ],
        out_specs=pl.BlockSpec((1, tm, d), lambda i, t: (i, t, 0)),
        out_shape=jax.ShapeDtypeStruct((b, nt * tm, d), F32),
        compiler_params=_cparams(("parallel", "parallel")),
        name="merge",
    )(xs, u_g, wkv_out[0], wkv_out[1], wkv_out[2], wkv_out[3], lru_out[0], lru_out[1], u_b, y_c,
      mod, p["rwkv_lnx_w"].reshape(1, -1), p["rwkv_lnx_b"].reshape(1, -1), e_bf, et_bf,
      p["w_branch"].astype(BF16), p["w_out"].astype(BF16), ng1)


def _ffn_kernel(x_ref, xp_ref, xn_ref, mod_ref, g2_ref, g3_ref, wi_ref, cw_ref, cb_ref, wo_ref, o_ref,
                *, tm, nctx, nt, dff):
    t = pl.program_id(1)
    first = jnp.logical_or(t == 0, t == nctx)
    last = jnp.logical_or(t == nctx - 1, t == nt - 1)
    nb = x_ref.shape[0]

    def hidden(xv, s):
        sh, sc = mod_ref[s, 0, 3:4, :], mod_ref[s, 0, 4:5, :]
        return (_rms(xv, g2_ref[...]) * (1.0 + sc) + sh).astype(BF16)

    h = jnp.concatenate([hidden(x_ref[s], s) for s in range(nb)], axis=0)
    h_halo = jnp.concatenate([hidden(ref[s], s) for s in range(nb) for ref in (xp_ref, xn_ref)], axis=0)
    gate = _dot(h, wi_ref[:, 0:dff])
    up = _dot(h, wi_ref[:, dff:2 * dff])
    g_halo = _dot(h_halo, wi_ref[:, 0:dff])
    rows = lax.broadcasted_iota(jnp.int32, (tm, 1), 0)
    acts = []
    for s in range(nb):
        g = gate[s * tm:(s + 1) * tm]
        g_prev = g_halo[(2 * s + 1) * FFN_HALO - 1:(2 * s + 1) * FFN_HALO, :]
        g_next = g_halo[(2 * s + 1) * FFN_HALO:(2 * s + 1) * FFN_HALO + 1, :]
        g_m1 = jnp.where(rows == 0, jnp.where(first, 0.0, g_prev), pltpu.roll(g, 1, 0))
        g_p1 = jnp.where(rows == tm - 1, jnp.where(last, 0.0, g_next), pltpu.roll(g, tm - 1, 0))
        gc = cw_ref[0:1, :] * g_m1 + cw_ref[1:2, :] * g + cw_ref[2:3, :] * g_p1 + cb_ref[...]
        acts.append((gc * _sigmoid(gc) * up[s * tm:(s + 1) * tm]).astype(BF16))
    out = _dot(jnp.concatenate(acts, axis=0), wo_ref[...])
    for s in range(nb):
        o_ref[s] = x_ref[s] + mod_ref[s, 0, 5:6, :] * _rms(out[s * tm:(s + 1) * tm], g3_ref[...])


def _ffn(x1, mod, p, ng2, ng3, lc):
    b, r, d = x1.shape
    tm = ROW_TILE
    nt = r // tm
    nctx = lc // tm
    hb = tm // FFN_HALO
    nhalo = r // FFN_HALO
    dff = p["ffn_w_out"].shape[0]
    nb = FFN_BATCHES if b % FFN_BATCHES == 0 else 1
    const2 = lambda i, t: (0, 0)
    resident = pl.Buffered(1)
    return pl.pallas_call(
        functools.partial(_ffn_kernel, tm=tm, nctx=nctx, nt=nt, dff=dff),
        grid=(b // nb, nt),
        in_specs=[
            pl.BlockSpec((nb, tm, d), lambda i, t: (i, t, 0)),
            pl.BlockSpec((nb, FFN_HALO, d), lambda i, t: (i, jnp.maximum(t * hb - 1, 0), 0)),
            pl.BlockSpec((nb, FFN_HALO, d), lambda i, t: (i, jnp.minimum((t + 1) * hb, nhalo - 1), 0)),
            pl.BlockSpec((nb, 1, 6, d), lambda i, t: (i, jnp.where(t < nctx, 0, 1), 0, 0)),
            pl.BlockSpec((1, d), const2),
            pl.BlockSpec((1, d), const2),
            pl.BlockSpec((d, 2 * dff), const2, pipeline_mode=resident),
            pl.BlockSpec((3, dff), const2),
            pl.BlockSpec((1, dff), const2),
            pl.BlockSpec((dff, d), const2, pipeline_mode=resident),
        ],
        out_specs=pl.BlockSpec((nb, tm, d), lambda i, t: (i, t, 0)),
        out_shape=jax.ShapeDtypeStruct((b, r, d), F32),
        compiler_params=_cparams(("parallel", "parallel")),
        name="conv_ffn",
    )(x1, x1, x1, mod, ng2, ng3, p["ffn_w_in"].astype(BF16), p["ffn_conv_w"],
      p["ffn_conv_b"].reshape(1, -1), p["ffn_w_out"].astype(BF16))


def _in_tile(r):
    for parts in (4, 8, 2, 1, 16, 32):
        if r % parts == 0 and (r // parts) % V7X_SUBLANES == 0 and r // parts <= 1152:
            return r // parts
    raise ValueError(f"no row tile for {r} rows")


def _col_tile(n):
    for parts in (1, 2, 3, 4, 6, 8):
        if n % parts == 0 and (n // parts) % V7X_LANES == 0 and n // parts <= 1792:
            return n // parts
    raise ValueError(f"no column tile for {n} columns")


def _layer(xs, mod, p, lc, update_ctx):
    width = p["rwkv_k_k"].shape[0]
    a_cols = 3 * width + W_LORA + A_LORA + G_LORA
    bounds = np.cumsum([0, a_cols, 2 * width, 3 * width, 3 * width])
    ng = p["norm_g"]
    w_in = p["w_in"].astype(BF16)
    tm = _in_tile(xs.shape[1])
    w_a, w_b, w_c, w_g = [w_in[:, lo:hi] for lo, hi in zip(bounds[:-1], bounds[1:])]
    mu_p, mu_n = p["rwkv_mu"][0], p["rwkv_mu"][1]
    shift_taps = jnp.stack([mu_p, 1.0 - mu_p - mu_n, mu_n, jnp.zeros_like(mu_p)])
    u_a = _in_proj_taps(xs, mod, ng[0:1], w_a, shift_taps, (-1, 0, 1), lc, tm, _col_tile(a_cols))
    ident = jnp.zeros((5, width), F32).at[2].set(1.0)
    conv_taps = jnp.concatenate(
        [jnp.concatenate([p["lru_conv_w"], p["lru_conv_b"][None]], axis=0), ident], axis=1)
    u_b = _in_proj_taps(xs, mod, ng[0:1], w_b, conv_taps, (-2, -1, 0, 1), lc, tm, _col_tile(2 * width))
    u_c = _in_proj(xs, mod, ng[0:1], w_c, lc, tm, _col_tile(3 * width))
    u_g = _in_proj(xs, mod, ng[0:1], w_g, lc, tm, _col_tile(3 * width))
    wkv_out, lru_out = _scans(u_a, u_b, p, lc)
    y_c = _na(u_c, p["na_rpb"], lc, update_ctx)
    x1 = _merge(xs, u_g, u_b, wkv_out, lru_out, y_c, mod, p, ng[1:2], lc, not update_ctx)
    return _ffn(x1, mod, p, ng[2:3], ng[3:4], lc if update_ctx else 0)


_PARAM_NAMES = ("norm_g", "w_in", "rwkv_mu", "rwkv_w0", "rwkv_w_up", "rwkv_a0", "rwkv_a_up", "rwkv_g_up",
                "rwkv_k_k", "rwkv_k_a", "rwkv_r_k", "rwkv_lnx_w", "rwkv_lnx_b", "lru_conv_w", "lru_conv_b",
                "lru_gate_a_w", "lru_gate_a_b", "lru_gate_x_w", "lru_gate_x_b", "lru_lambda", "na_rpb",
                "w_branch", "w_out", "ffn_w_in", "ffn_conv_w", "ffn_conv_b", "ffn_w_out")


def kernel(x, c, ctx, c_ctx, ada_w, ada_b, norm_g, w_in, rwkv_mu, rwkv_w0, rwkv_w_up, rwkv_a0, rwkv_a_up, rwkv_g_up, rwkv_k_k, rwkv_k_a, rwkv_r_k, rwkv_lnx_w, rwkv_lnx_b, lru_conv_w, lru_conv_b, lru_gate_a_w, lru_gate_a_b, lru_gate_x_w, lru_gate_x_b, lru_lambda, na_rpb, w_branch, w_out, ffn_w_in, ffn_conv_w, ffn_conv_b, ffn_w_out):
    stacked = dict(zip(_PARAM_NAMES, (norm_g, w_in, rwkv_mu, rwkv_w0, rwkv_w_up, rwkv_a0, rwkv_a_up,
                                      rwkv_g_up, rwkv_k_k, rwkv_k_a, rwkv_r_k, rwkv_lnx_w, rwkv_lnx_b,
                                      lru_conv_w, lru_conv_b, lru_gate_a_w, lru_gate_a_b, lru_gate_x_w,
                                      lru_gate_x_b, lru_lambda, na_rpb, w_branch, w_out, ffn_w_in,
                                      ffn_conv_w, ffn_conv_b, ffn_w_out)))
    depth = ada_w.shape[0]
    b, _, d = x.shape
    lc = ctx.shape[1]
    pad = (-(b + 1)) % V7X_SUBLANES
    cc = jnp.concatenate([c, c_ctx[None, :], jnp.zeros((pad, d), c.dtype)], axis=0)
    mod_all = _modulation(cc, ada_w, ada_b)
    xs = jnp.concatenate([ctx, x], axis=1)
    for l in range(depth):
        p = {name: val[l] for name, val in stacked.items()}
        p["rwkv_r_k"] = p["rwkv_r_k"].reshape(-1)
        m = mod_all[l].reshape(-1, 6, d)
        mod = jnp.stack([jnp.broadcast_to(m[b], (b, 6, d)), m[:b]], axis=1)
        xs = _layer(xs, mod, p, lc, l < depth - 1)
    return xs
```

```python
import functools

import numpy as np
import jax
import jax.numpy as jnp
from jax import lax
from jax.experimental import pallas as pl
from jax.experimental.pallas import tpu as pltpu

F32 = jnp.float32
BF16 = jnp.bfloat16
ACT_DTYPE = BF16

GRID_W = 64
HEAD_DIM = 64
W_LORA = 64
A_LORA = 64
G_LORA = 128
LNX_EPS = 64e-5
LRU_C = 8.0
NA_KH = 8
NA_KW = 16
NA_QROWS = 2
NA_BLOCKS_PER_ITER = 4
NA_BAND_ROWS = 10
ROPE_BASE = 10000.0
EPS = 1e-6
NEG_INF = -1e30

V7X_LANES = 128
V7X_SUBLANES = 8
V7X_VMEM_LIMIT_BYTES = 56 * 1024 * 1024

PAIR_W = 2 * HEAD_DIM
SCAN_CHUNK = 64
WKV_DECAY_SCALE = float(np.exp(-0.5))
ROW_TILE = 256
LRU_GROUP = 256
IN_HALO = 16
FFN_BATCHES = 4
FFN_HALO = 16


def _cparams(sem):
    return pltpu.CompilerParams(dimension_semantics=sem, vmem_limit_bytes=V7X_VMEM_LIMIT_BYTES)


def _split_bf16(x, parts):
    out = []
    rem = x
    for _ in range(parts - 1):
        hi = rem.astype(BF16)
        out.append(hi)
        rem = rem - hi.astype(F32)
    out.append(rem.astype(BF16))
    return out


def _dot(a, b):
    return jnp.dot(a, b, preferred_element_type=F32)


def _dot_nt(a, b):
    return lax.dot_general(a, b, (((1,), (1,)), ((), ())), preferred_element_type=F32)


def _dot_tn(a, b):
    return lax.dot_general(a, b, (((0,), (0,)), ((), ())), preferred_element_type=F32)


def _dot_exact_lhs(a_bf16, x, parts):
    acc = None
    for piece in _split_bf16(x, parts):
        t = _dot(a_bf16, piece)
        acc = t if acc is None else acc + t
    return acc


def _dot_exact_rhs(x, b_bf16, parts):
    acc = None
    for piece in _split_bf16(x, parts):
        t = _dot(piece, b_bf16)
        acc = t if acc is None else acc + t
    return acc


def _sigmoid(x):
    return 1.0 / (1.0 + jnp.exp(-x))


def _softplus(x):
    return jnp.maximum(x, 0.0) + jnp.log(1.0 + jnp.exp(-jnp.abs(x)))


def _rms(x, g):
    return x * lax.rsqrt(jnp.mean(x * x, axis=-1, keepdims=True) + EPS) * g


def _mod_kernel(c_ref, w_ref, b_ref, o_ref):
    c = c_ref[...]
    s = c * _sigmoid(c)
    o_ref[0] = _dot(s.astype(BF16), w_ref[0].astype(BF16)) + b_ref[0]


def _modulation(cc, ada_w, ada_b):
    depth, d, n = ada_w.shape
    rows = cc.shape[0]
    tn = n // 4
    return pl.pallas_call(
        _mod_kernel,
        grid=(depth, n // tn),
        in_specs=[
            pl.BlockSpec((rows, d), lambda l, j: (0, 0)),
            pl.BlockSpec((1, d, tn), lambda l, j: (l, 0, j)),
            pl.BlockSpec((1, 1, tn), lambda l, j: (l, 0, j)),
        ],
        out_specs=pl.BlockSpec((1, rows, tn), lambda l, j: (l, 0, j)),
        out_shape=jax.ShapeDtypeStruct((depth, rows, n), F32),
        compiler_params=_cparams(("parallel", "parallel")),
        name="modulation",
    )(cc, ada_w, ada_b.reshape(depth, 1, n))


def _in_kernel(x_ref, mod_ref, g_ref, w_ref, o_ref, hb_ref, *, lc, tm):
    @pl.when(pl.program_id(2) == 0)
    def _():
        y = _rms(x_ref[0], g_ref[...])
        row = pl.program_id(1) * tm + lax.broadcasted_iota(jnp.int32, (tm, 1), 0)
        is_ctx = row < lc
        sh = jnp.where(is_ctx, mod_ref[0, 0, 0:1, :], mod_ref[0, 1, 0:1, :])
        sc = jnp.where(is_ctx, mod_ref[0, 0, 1:2, :], mod_ref[0, 1, 1:2, :])
        hb_ref[...] = (y * (1.0 + sc) + sh).astype(BF16)

    o_ref[0] = _dot(hb_ref[...], w_ref[...]).astype(o_ref.dtype)


def _in_taps_kernel(x_ref, xp_ref, xn_ref, mod_ref, g_ref, w_ref, taps_ref, o_ref, hb_ref,
                    *, lc, r, tm, shifts):
    t = pl.program_id(1)
    halo = IN_HALO
    ns = V7X_SUBLANES

    @pl.when(pl.program_id(2) == 0)
    def _():
        def hidden(xv, row0):
            row = row0 + lax.broadcasted_iota(jnp.int32, (xv.shape[0], 1), 0)
            is_ctx = row < lc
            sh = jnp.where(is_ctx, mod_ref[0, 0, 0:1, :], mod_ref[0, 1, 0:1, :])
            sc = jnp.where(is_ctx, mod_ref[0, 0, 1:2, :], mod_ref[0, 1, 1:2, :])
            return (_rms(xv, g_ref[...]) * (1.0 + sc) + sh).astype(BF16)

        hb_ref[0:halo, :] = hidden(xp_ref[0], t * tm - halo)
        hb_ref[halo:halo + tm, :] = hidden(x_ref[0], t * tm)
        hb_ref[halo + tm:2 * halo + tm, :] = hidden(xn_ref[0], (t + 1) * tm)

    tn = o_ref.shape[2]
    lead = halo // ns
    sub_row = lax.broadcasted_iota(jnp.int32, (1, ns, 1), 1)
    tiles = tm // ns
    u = _dot(hb_ref[...], w_ref[...]).reshape(tiles + 2 * lead, ns, tn)
    row = (t * tm + ns * lax.broadcasted_iota(jnp.int32, (tiles, 1, 1), 0)
           + lax.broadcasted_iota(jnp.int32, (1, ns, 1), 1))
    from_start = row - jnp.where(row < lc, 0, lc)
    to_end = jnp.where(row < lc, lc, r) - 1 - row
    acc = None
    for j, s in enumerate(shifts):
        coef = taps_ref[j:j + 1, :].reshape(1, 1, tn)
        if s == 0:
            term = u[lead:lead + tiles]
        elif s < 0:
            rolled = pltpu.roll(u[lead - 1:lead + tiles], -s, 1)
            term = jnp.where(sub_row >= -s, rolled[1:], rolled[:-1])
            term = jnp.where(from_start >= -s, term, 0.0)
        else:
            rolled = pltpu.roll(u[lead:lead + tiles + 1], ns - s, 1)
            term = jnp.where(sub_row < ns - s, rolled[:-1], rolled[1:])
            term = jnp.where(to_end >= s, term, 0.0)
        acc = coef * term if acc is None else acc + coef * term
    acc = acc + taps_ref[len(shifts):len(shifts) + 1, :].reshape(1, 1, tn)
    o_ref[0] = acc.reshape(tm, tn).astype(o_ref.dtype)


def _in_proj_taps(xs, mod, g, w, taps, shifts, lc, tm, tn):
    b, r, d = xs.shape
    n = w.shape[1]
    hb = tm // IN_HALO
    nhalo = r // IN_HALO
    return pl.pallas_call(
        functools.partial(_in_taps_kernel, lc=lc, r=r, tm=tm, shifts=shifts),
        grid=(b, r // tm, n // tn),
        in_specs=[
            pl.BlockSpec((1, tm, d), lambda i, t, j: (i, t, 0)),
            pl.BlockSpec((1, IN_HALO, d), lambda i, t, j: (i, jnp.maximum(t * hb - 1, 0), 0)),
            pl.BlockSpec((1, IN_HALO, d), lambda i, t, j: (i, jnp.minimum((t + 1) * hb, nhalo - 1), 0)),
            pl.BlockSpec((1, 2, 6, d), lambda i, t, j: (i, 0, 0, 0)),
            pl.BlockSpec((1, d), lambda i, t, j: (0, 0)),
            pl.BlockSpec((d, tn), lambda i, t, j: (0, j)),
            pl.BlockSpec((len(shifts) + 1, tn), lambda i, t, j: (0, j)),
        ],
        out_specs=pl.BlockSpec((1, tm, tn), lambda i, t, j: (i, t, j)),
        out_shape=jax.ShapeDtypeStruct((b, r, n), F32),
        scratch_shapes=[pltpu.VMEM((tm + 2 * IN_HALO, d), BF16)],
        compiler_params=_cparams(("parallel", "parallel", "arbitrary")),
        name="in_proj_taps",
    )(xs, xs, xs, mod, g, w, taps)


def _in_proj(xs, mod, g, w, lc, tm, tn):
    b, r, d = xs.shape
    n = w.shape[1]
    return pl.pallas_call(
        functools.partial(_in_kernel, lc=lc, tm=tm),
        grid=(b, r // tm, n // tn),
        in_specs=[
            pl.BlockSpec((1, tm, d), lambda i, t, j: (i, t, 0)),
            pl.BlockSpec((1, 2, 6, d), lambda i, t, j: (i, 0, 0, 0)),
            pl.BlockSpec((1, d), lambda i, t, j: (0, 0)),
            pl.BlockSpec((d, tn), lambda i, t, j: (0, j)),
        ],
        out_specs=pl.BlockSpec((1, tm, tn), lambda i, t, j: (i, t, j)),
        out_shape=jax.ShapeDtypeStruct((b, r, n), ACT_DTYPE),
        scratch_shapes=[pltpu.VMEM((tm, d), BF16)],
        compiler_params=_cparams(("parallel", "parallel", "arbitrary")),
        name="in_proj",
    )(xs, mod, g, w)


def _stack_pair(x, lane_a):
    return jnp.concatenate([jnp.where(lane_a, x, 0.0), jnp.where(lane_a, 0.0, x)], axis=0)


def _wkv_chains(chains, c):
    n = 2 * c
    lane_a = lax.broadcasted_iota(jnp.int32, (c, PAIR_W), 1) < HEAD_DIM
    stack = lambda x: _stack_pair(x.astype(BF16), lane_a)
    nch = len(chains)

    ops = []
    for ch in chains:
        r, kd, v, kkn, a, lw, cum = ch[:7]
        tot = jnp.sum(lw, axis=0, keepdims=True)
        e_inv = jnp.exp(-cum)
        e_rem = jnp.exp(tot - cum)
        akk = a * kkn
        lhs = jnp.concatenate([stack(-kkn * jnp.exp(cum - lw)), stack(r * jnp.exp(cum))], axis=0)
        bk = jnp.concatenate([stack(akk * e_inv), stack(kd * e_inv)], axis=0)
        bkc = jnp.concatenate([stack(akk * e_rem), stack(kd * e_rem)], axis=0)
        ops.append((lhs, bk, bkc, stack(v), tot))

    g = [_dot_nt(o[0], o[1]) for o in ops]
    ls = [_dot_nt(o[0], ch[7].astype(BF16)) for o, ch in zip(ops, chains)]
    m_b = [jnp.where(ch[8], gi[:n, :n], 0.0) for ch, gi in zip(chains, g)]
    m_k = [jnp.where(ch[8], gi[:n, n:], 0.0).astype(BF16) for ch, gi in zip(chains, g)]
    n_bk = [jnp.concatenate([jnp.where(ch[9], gi[n:, :n], 0.0).astype(BF16),
                             jnp.where(ch[9], gi[n:, n:], 0.0).astype(BF16)], axis=1)
            for ch, gi in zip(chains, g)]
    mkv = [_dot(mk, o[3]) for mk, o in zip(m_k, ops)]

    eye = (lax.broadcasted_iota(jnp.int32, (n, n), 0) == lax.broadcasted_iota(jnp.int32, (n, n), 1))
    t_inv = [jnp.where(eye, 1.0, 0.0) + m for m in m_b]
    m_pow = [m.astype(BF16) for m in m_b]
    span = 1
    while 2 * span < c:
        m_pow = [_dot(mp, mp).astype(BF16) for mp in m_pow]
        t_inv = [t + _dot(t.astype(BF16), mp) for t, mp in zip(t_inv, m_pow)]
        span *= 2

    u = [_dot(t.astype(BF16), (l[:n] + mv).astype(BF16)).astype(BF16)
         for t, l, mv in zip(t_inv, ls, mkv)]
    uv = [jnp.concatenate([ui, o[3]], axis=0) for ui, o in zip(u, ops)]
    ys = [l[n:] + _dot(nb, x) for l, nb, x in zip(ls, n_bk, uv)]
    s_new = [ch[7] * jnp.exp(o[4]) + _dot_tn(x, o[2]) for ch, o, x in zip(chains, ops, uv)]
    return [(ys[j][:c] + ys[j][c:], s_new[j]) for j in range(nch)]


def _scan_kernel(uf_ref, ub_ref, xf_ref, xb_ref,
                 w0_ref, wup_ref, a0_ref, aup_ref, gup_ref, kk_ref, ka_ref, rk_ref,
                 tri_ref, ms_ref, mi_ref, bd_ref,
                 wa_ref, ba_ref, wx_ref, bx_ref, lam_ref,
                 yf_ref, yb_ref, eb_ref, eg_ref, hf_ref, hb_ref, s_ref, h_ref, *, c):
    width = yf_ref.shape[2]
    npair = width // PAIR_W

    @pl.when(pl.program_id(1) == 0)
    def _():
        s_ref[...] = jnp.zeros(s_ref.shape, F32)
        h_ref[...] = jnp.zeros(h_ref.shape, F32)

    lru_gates = [_lru_gates(x_ref, wa_ref, wx_ref, d) for d, x_ref in enumerate((xf_ref, xb_ref))]

    us = (uf_ref[0], ub_ref[0])
    o = 3 * width
    w_lo = [jnp.tanh(u[:, o:o + W_LORA]).astype(BF16) for u in us]
    a_lo = [u[:, o + W_LORA:o + W_LORA + A_LORA].astype(BF16) for u in us]
    wpre = [w0_ref[d:d + 1, :] + _dot(w_lo[d], wup_ref[d]) for d in range(2)]
    apre = [a0_ref[d:d + 1, :] + _dot(a_lo[d], aup_ref[d]) for d in range(2)]
    apre_rev = a0_ref[1:2, :] + _dot(a_lo[0], aup_ref[1])
    g_lo = _sigmoid(us[0][:, o + W_LORA + A_LORA:o + W_LORA + A_LORA + G_LORA]).astype(BF16)
    eg_ref[0] = _dot(g_lo, gup_ref[...])

    bd = bd_ref[...]
    masks = [(ms_ref[d] > 0.5, mi_ref[d] > 0.5) for d in range(2)]
    def cols(d, part, p):
        return us[d][:, part * width + p * PAIR_W:part * width + (p + 1) * PAIR_W]

    lw, cum, kkraw, ss = {}, {}, {}, {}
    for p in range(npair):
        sl = slice(p * PAIR_W, (p + 1) * PAIR_W)
        for d in range(2):
            lw[d, p] = -WKV_DECAY_SCALE * _sigmoid(wpre[d][:, sl])
            cum[d, p] = _dot_exact_lhs(tri_ref[d], lw[d, p], 2)
            kkraw[d, p] = cols(d, 1, p) * kk_ref[:, sl]
            ss[d, p] = _dot((kkraw[d, p] * kkraw[d, p]).astype(BF16), bd)

    keys = [(d, p) for p in range(npair) for d in range(2)]
    chains, rkk = [], []
    for d, p in keys:
        sl = slice(p * PAIR_W, (p + 1) * PAIR_W)
        r, k, v = cols(d, 0, p), cols(d, 1, p), cols(d, 2, p)
        a = _sigmoid(apre[d][:, sl])
        kd = k * (1.0 + (a - 1.0) * ka_ref[:, sl])
        kkn = kkraw[d, p] * lax.rsqrt(jnp.maximum(ss[d, p], 1e-24))
        chains.append((r, kd, v, kkn, a, lw[d, p], cum[d, p], s_ref[d, p]) + masks[d])
        if d == 0:
            kd_rev = k * (1.0 + (_sigmoid(apre_rev[:, sl]) - 1.0) * ka_ref[:, sl])
            rkk.append(_split_bf16(r * (kd + kd_rev) * rk_ref[:, sl], 2))

    rk_hi = [_dot(s[0], bd) for s in rkk]
    rk_lo = [_dot(s[1], bd) for s in rkk]
    for (d, p), (y, s_new) in zip(keys, _wkv_chains(chains, c)):
        s_ref[d, p] = s_new
        (yf_ref, yb_ref)[d][0, :, p * PAIR_W:(p + 1) * PAIR_W] = y
    for p in range(npair):
        eb_ref[0, :, p * PAIR_W:(p + 1) * PAIR_W] = (rk_hi[p] + rk_lo[p]) * cols(0, 2, p)

    for d, (x_ref, out_ref) in enumerate(((xf_ref, hf_ref), (xb_ref, hb_ref))):
        _lru_step(x_ref, lru_gates[d], ba_ref[d:d + 1, :], bx_ref[d:d + 1, :], lam_ref, out_ref, h_ref, c, d)


def _wkv_consts(c):
    t = np.arange(c)
    lower = (t[:, None] >= t[None, :])
    tri = np.stack([lower, lower.T]).astype(np.float32)
    strict = np.stack([t[:, None] > t[None, :], t[:, None] < t[None, :]])
    incl = np.stack([lower, lower.T])
    ms = np.tile(strict, (1, 2, 2)).astype(np.float32)
    mi = np.tile(incl, (1, 2, 2)).astype(np.float32)
    h = np.arange(PAIR_W) // HEAD_DIM
    bd = (h[:, None] == h[None, :]).astype(np.float32)
    return jnp.asarray(tri, BF16), jnp.asarray(ms), jnp.asarray(mi), jnp.asarray(bd, BF16)


def _scans(u_a, u_b, p, lc):
    b, r, fa = u_a.shape
    width = p["rwkv_k_k"].shape[0]
    c = SCAN_CHUNK
    nc, nl = lc // c, (r - lc) // c
    nchunk = nc + nl
    npair = width // PAIR_W
    ng = width // LRU_GROUP
    tri, ms, mi, bd = _wkv_consts(c)
    const4 = lambda bi, i: (0, 0, 0, 0)

    def rev(i):
        return jnp.where(i < nc, nc - 1 - i, 2 * nc + nl - 1 - i)

    cur_f = lambda bi, i: (bi, i, 0)
    cur_b = lambda bi, i: (bi, rev(i), 0)
    const2 = lambda bi, i: (0, 0)
    const3 = lambda bi, i: (0, 0, 0)

    row = lambda x: x.reshape(1, -1)
    out_sd = jax.ShapeDtypeStruct((b, r, width), F32)
    outs = pl.pallas_call(
        functools.partial(_scan_kernel, c=c),
        grid=(b, nchunk),
        in_specs=[
            pl.BlockSpec((1, c, fa), cur_f),
            pl.BlockSpec((1, c, fa), cur_b),
            pl.BlockSpec((1, c, width), cur_f),
            pl.BlockSpec((1, c, width), cur_b),
            pl.BlockSpec((2, width), const2),
            pl.BlockSpec((2, W_LORA, width), const3),
            pl.BlockSpec((2, width), const2),
            pl.BlockSpec((2, A_LORA, width), const3),
            pl.BlockSpec((G_LORA, width), const2),
            pl.BlockSpec((1, width), const2),
            pl.BlockSpec((1, width), const2),
            pl.BlockSpec((1, width), const2),
            pl.BlockSpec((2, c, c), const3),
            pl.BlockSpec((2, 2 * c, 2 * c), const3),
            pl.BlockSpec((2, 2 * c, 2 * c), const3),
            pl.BlockSpec((PAIR_W, PAIR_W), const2),
            pl.BlockSpec((2, ng, LRU_GROUP, LRU_GROUP), const4),
            pl.BlockSpec((2, width), const2),
            pl.BlockSpec((2, ng, LRU_GROUP, LRU_GROUP), const4),
            pl.BlockSpec((2, width), const2),
            pl.BlockSpec((2, width), const2),
        ],
        out_specs=[
            pl.BlockSpec((1, c, width), cur_f),
            pl.BlockSpec((1, c, width), cur_b),
            pl.BlockSpec((1, c, width), cur_f),
            pl.BlockSpec((1, c, width), cur_f),
            pl.BlockSpec((1, c, width), cur_f),
            pl.BlockSpec((1, c, width), cur_b),
        ],
        out_shape=[out_sd] * 6,
        scratch_shapes=[
            pltpu.VMEM((2, npair, PAIR_W, PAIR_W), F32),
            pltpu.VMEM((2, width), F32),
        ],
        compiler_params=_cparams(("parallel", "arbitrary")),
        name="scans",
    )(u_a, u_a, u_b, u_b,
      p["rwkv_w0"], p["rwkv_w_up"].astype(BF16), p["rwkv_a0"],
      p["rwkv_a_up"].astype(BF16), p["rwkv_g_up"].astype(BF16),
      row(p["rwkv_k_k"]), row(p["rwkv_k_a"]), row(p["rwkv_r_k"]),
      tri, ms, mi, bd,
      _block_diag_groups(p["lru_gate_a_w"]), p["lru_gate_a_b"],
      _block_diag_groups(p["lru_gate_x_w"]), p["lru_gate_x_b"], p["lru_lambda"])
    return outs[:4], outs[4:]


def _lru_scan(a, bv, h0, out_ref, c, reverse):
    ns = V7X_SUBLANES
    tiles = c // ns
    width = a.shape[1]
    a = a.reshape(tiles, ns, width)
    bv = bv.reshape(tiles, ns, width)
    sub_row = lax.broadcasted_iota(jnp.int32, (1, ns, 1), 1)
    s = 1
    while s < ns:
        keep = sub_row < ns - s if reverse else sub_row >= s
        shift = ns - s if reverse else s
        a_sh = jnp.where(keep, pltpu.roll(a, shift, 1), 1.0)
        b_sh = jnp.where(keep, pltpu.roll(bv, shift, 1), 0.0)
        bv = a * b_sh + bv
        a = a * a_sh
        s *= 2
    state = h0
    for k in (range(tiles - 1, -1, -1) if reverse else range(tiles)):
        hk = bv[k] + a[k] * state
        out_ref[0, k * ns:(k + 1) * ns, :] = hk
        state = hk[0:1, :] if reverse else hk[ns - 1:ns, :]
    return state


def _lru_gates(u_ref, wa_ref, wx_ref, d):
    xcb = u_ref[0].astype(BF16)
    ng = xcb.shape[1] // LRU_GROUP
    ga = jnp.concatenate([_dot(xcb[:, g * LRU_GROUP:(g + 1) * LRU_GROUP], wa_ref[d, g])
                          for g in range(ng)], axis=1)
    gx = jnp.concatenate([_dot(xcb[:, g * LRU_GROUP:(g + 1) * LRU_GROUP], wx_ref[d, g])
                          for g in range(ng)], axis=1)
    return ga, gx


def _lru_step(u_ref, gates, bias_a, bias_x, lam_ref, out_ref, h_ref, c, d):
    ga, gx = gates
    rg = _sigmoid(ga + bias_a)
    ig = _sigmoid(gx + bias_x)
    a = jnp.exp(-LRU_C * rg * _softplus(-lam_ref[d:d + 1, :]))
    bv = jnp.sqrt(1.0 - a * a) * (ig * u_ref[0])
    h_ref[d:d + 1, :] = _lru_scan(a, bv, h_ref[d:d + 1, :], out_ref, c, d == 1)


def _block_diag_groups(w):
    nd, nb, bw, _ = w.shape
    per = LRU_GROUP // bw
    w = w.reshape(nd, nb // per, per, bw, bw)
    eye = jnp.eye(per, dtype=w.dtype)
    out = jnp.einsum("dgpij,pq->dgpiqj", w, eye)
    return out.reshape(nd, nb // per, LRU_GROUP, LRU_GROUP).astype(BF16)


def _na_geometry(rows):
    assert rows >= NA_BAND_ROWS
    kh = min(NA_KH, rows)
    kbh = NA_BAND_ROWS
    deltas = []
    for blk in range(rows // NA_QROWS):
        r0 = blk * NA_QROWS
        kr = int(np.clip(r0 - kh // 2, 0, rows - kbh))
        deltas.append(r0 - kr)
    return kh, kbh, deltas, sorted(set(deltas))


def _na_bias_table(rpb, rows):
    w = GRID_W
    kh, kbh, deltas, geos = _na_geometry(rows)
    nq, nk = NA_QROWS * w, kbh * w
    drow = np.zeros((len(geos), nq, nk), np.int32)
    dcol = np.zeros((len(geos), nq, nk), np.int32)
    ok = np.zeros((len(geos), nq, nk), bool)
    qc = np.tile(np.arange(w), NA_QROWS)
    qr = np.repeat(np.arange(NA_QROWS), w)
    kc = np.tile(np.arange(w), kbh)
    kj = np.repeat(np.arange(kbh), w)
    cs = np.clip(qc - NA_KW // 2, 0, w - NA_KW)
    col_ok = (kc[None, :] >= cs[:, None]) & (kc[None, :] < cs[:, None] + NA_KW)
    dc = np.clip(kc[None, :] - qc[:, None] + NA_KW - 1, 0, 2 * NA_KW - 2)
    for gi, delta in enumerate(geos):
        seen = None
        for blk, dl in enumerate(deltas):
            if dl != delta:
                continue
            r0 = blk * NA_QROWS
            kr = r0 - delta
            qrow = r0 + qr
            krow = kr + kj
            rs = np.clip(qrow - kh // 2, 0, rows - kh)
            row_ok = (krow[None, :] >= rs[:, None]) & (krow[None, :] < rs[:, None] + kh)
            dr = np.clip(krow[None, :] - qrow[:, None] + NA_KH - 1, 0, 2 * NA_KH - 2)
            cur = (row_ok & col_ok, dr)
            if seen is None:
                seen = cur
            else:
                assert np.array_equal(seen[0], cur[0]) and np.array_equal(seen[1], cur[1])
        ok[gi], drow[gi], dcol[gi] = seen[0], seen[1], dc
    dr5 = drow.reshape(len(geos), NA_QROWS, w, kbh, w)[:, :, 0, :, 0]
    dc4 = dc.reshape(NA_QROWS, w, kbh, w)[0, :, 0, :]
    oh_r = (dr5[..., None] == np.arange(rpb.shape[1])).astype(np.float32)
    oh_c = (dc4[None] == np.arange(rpb.shape[2])[:, None, None]).astype(np.float32)
    h = rpb.shape[0]
    rpb2 = rpb.reshape(h // 2, 2, rpb.shape[1], rpb.shape[2])
    rows_sel = jnp.einsum("gqjr,phrc->pghqjc", oh_r, rpb2, precision=lax.Precision.HIGHEST)
    bias = jnp.einsum("pghqjc,cxy->pghqxjy", rows_sel, oh_c, precision=lax.Precision.HIGHEST)
    bias = bias.reshape(h // 2, len(geos), 2, nq, nk)
    bias = jnp.where(ok[None, :, None], bias, NEG_INF)
    return bias, deltas, geos


def _rope_tables(t):
    half = HEAD_DIM // 2
    freqs = ROPE_BASE ** (-jnp.arange(0, half, 2, dtype=F32) / half)
    pos = jnp.arange(t)
    prow, pcol = pos // GRID_W, pos % GRID_W
    ang_r = prow.astype(F32)[:, None] * freqs[None, :]
    ang_c = pcol.astype(F32)[:, None] * freqs[None, :]
    cos = jnp.concatenate([jnp.cos(ang_r)] * 2 + [jnp.cos(ang_c)] * 2, axis=-1)
    sin = jnp.concatenate([-jnp.sin(ang_r), jnp.sin(ang_r), -jnp.sin(ang_c), jnp.sin(ang_c)], axis=-1)
    return jnp.tile(cos, (1, 2)), jnp.tile(sin, (1, 2))


def _na_kernel(q_ref, k_ref, v_ref, cos_ref, sin_ref, swap_ref, bias_ref, geo_ref, o_ref,
               qr_s, qp_s, kr_s, v_s, *, lc, rows, kbh, with_ctx):
    w = GRID_W
    nq, nk = NA_QROWS * w, kbh * w
    scale = HEAD_DIM ** -0.5

    def rope(x):
        return x * cos_ref[...] + _dot_exact_rhs(x, swap_ref[...], 2) * sin_ref[...]

    assert np.log2(HEAD_DIM) % 2 == 0
    qr_s[...] = (rope(q_ref[0, lc:, :].astype(F32)) * scale).astype(BF16)
    qp_s[...] = (q_ref[0].astype(F32) * scale).astype(BF16)
    kr_s[...] = rope(k_ref[0, lc:, :].astype(F32)).astype(BF16)
    v_s[...] = v_ref[0].astype(BF16)
    k_ctx = k_ref[0, :lc, :].astype(BF16)

    lane_a = lax.broadcasted_iota(jnp.int32, (nq, PAIR_W), 1) < HEAD_DIM
    zero = jnp.zeros((), BF16)

    def by_head(q):
        return jnp.concatenate([jnp.where(lane_a, q, zero), jnp.where(lane_a, zero, q)], axis=0)

    def attend(chains):
        scores = [[_dot_nt(q, k) if b is None else _dot_nt(q, k) + b
                   for q, k, b, _ in parts] for parts in chains]
        mx = [functools.reduce(jnp.maximum, [jnp.max(s, axis=-1, keepdims=True) for s in sc])
              for sc in scores]
        es = [[jnp.exp(s - m) for s in sc] for sc, m in zip(scores, mx)]
        den = [functools.reduce(lambda a, b: a + b, [jnp.sum(e, axis=-1, keepdims=True) for e in ee])
               for ee in es]
        acc = [functools.reduce(lambda a, b: a + b,
                                [_dot(e.astype(BF16), part[3]) for e, part in zip(ee, parts)])
               for ee, parts in zip(es, chains)]
        return [a / d for a, d in zip(acc, den)]

    def blocks(it, carry):
        chains, q0s = [], []
        v_ctx = v_s[0:lc, :]
        for j in range(NA_BLOCKS_PER_ITER):
            blk = it * NA_BLOCKS_PER_ITER + j
            r0 = blk * NA_QROWS
            kr = jnp.clip(r0 - min(NA_KH, rows) // 2, 0, rows - kbh)
            geo = geo_ref[blk]
            q0 = pl.multiple_of(r0 * w, w)
            k0 = pl.multiple_of(kr * w, w)
            qb = qr_s[pl.ds(q0, nq), :]
            qpb = qp_s[pl.ds(lc + q0, nq), :]
            kb = kr_s[pl.ds(k0, nk), :]
            vb = v_s[pl.ds(lc + k0, nk), :]
            q0s.append(q0)
            chains.append([(by_head(qb), kb, bias_ref[0, geo].reshape(2 * nq, nk), vb),
                           (by_head(qpb), k_ctx, None, v_ctx)])
        outs = attend(chains)
        for q0, o in zip(q0s, outs):
            o_ref[0, pl.ds(lc + q0, nq), :] = jnp.where(lane_a, o[:nq], o[nq:]).astype(o_ref.dtype)
        return carry

    lax.fori_loop(0, rows // NA_QROWS // NA_BLOCKS_PER_ITER, blocks, 0)

    if with_ctx:
        v_ctx = v_s[0:lc, :]
        nblk = lc // nq
        outs = attend([[(by_head(qp_s[cblk * nq:(cblk + 1) * nq, :]), k_ctx, None, v_ctx)]
                       for cblk in range(nblk)])
        for cblk, o in enumerate(outs):
            o_ref[0, cblk * nq:(cblk + 1) * nq, :] = jnp.where(lane_a, o[:nq], o[nq:]).astype(o_ref.dtype)
    else:
        o_ref[0, 0:lc, :] = jnp.zeros((lc, PAIR_W), o_ref.dtype)


def _na(u_c, width, rpb, lc, with_ctx):
    b, r, _ = u_c.shape
    npair = width // PAIR_W
    t = r - lc
    rows = t // GRID_W
    bias, deltas, geos = _na_bias_table(rpb, rows)
    kh, kbh, _, _ = _na_geometry(rows)
    geo_idx = jnp.asarray([geos.index(d) for d in deltas], jnp.int32)
    cos, sin = _rope_tables(t)
    quarter = HEAD_DIM // 4
    lane = np.arange(PAIR_W)
    src = np.where(lane % (2 * quarter) < quarter, lane + quarter, lane - quarter)
    swap = jnp.asarray((lane[:, None] == src[None, :]).astype(np.float32), BF16)
    nq, nk = NA_QROWS * GRID_W, kbh * GRID_W
    return pl.pallas_call(
        functools.partial(_na_kernel, lc=lc, rows=rows, kbh=kbh, with_ctx=with_ctx),
        grid=(npair, b),
        in_specs=[
            pl.BlockSpec((1, r, PAIR_W), lambda p, i: (i, 0, p)),
            pl.BlockSpec((1, r, PAIR_W), lambda p, i: (i, 0, npair + p)),
            pl.BlockSpec((1, r, PAIR_W), lambda p, i: (i, 0, 2 * npair + p)),
            pl.BlockSpec((t, PAIR_W), lambda p, i: (0, 0)),
            pl.BlockSpec((t, PAIR_W), lambda p, i: (0, 0)),
            pl.BlockSpec((PAIR_W, PAIR_W), lambda p, i: (0, 0)),
            pl.BlockSpec((1, len(geos), 2, nq, nk), lambda p, i: (p, 0, 0, 0, 0)),
            pl.BlockSpec(memory_space=pltpu.SMEM),
        ],
        out_specs=pl.BlockSpec((1, r, PAIR_W), lambda p, i: (i, 0, p)),
        out_shape=jax.ShapeDtypeStruct((b, r, width), ACT_DTYPE),
        scratch_shapes=[
            pltpu.VMEM((t, PAIR_W), BF16),
            pltpu.VMEM((r, PAIR_W), BF16),
            pltpu.VMEM((t, PAIR_W), BF16),
            pltpu.VMEM((r, PAIR_W), BF16),
        ],
        compiler_params=_cparams(("parallel", "parallel")),
        name="natten",
    )(u_c, u_c, u_c, cos, sin, swap, bias, geo_idx)


def _merge_kernel(x_ref, ug_ref, yf_ref, yb_ref, eb_ref, eg_ref, hf_ref, hb_ref, gb_ref, yc_ref,
                  mod_ref, lw_ref, lb_ref, e_ref, et_ref, wbr_ref, wo_ref, g_ref, o_ref):
    width = yf_ref.shape[2]

    def group_mean(z):
        s = _dot_exact_rhs(z, e_ref[...], 2)
        return _dot_exact_rhs(s, et_ref[...], 2) * (1.0 / HEAD_DIM)

    y = yf_ref[0] + yb_ref[0]
    dev = y - group_mean(y)
    yn = dev * lax.rsqrt(group_mean(dev * dev) + LNX_EPS)
    y_a = (yn * lw_ref[...] + lb_ref[...] + eb_ref[0]) * eg_ref[0]
    y_b = (hf_ref[0] + hb_ref[0]) * jax.nn.gelu(gb_ref[0])
    ug = ug_ref[0].astype(F32)
    m = (_sigmoid(ug[:, 0:width]) * _dot(y_a.astype(BF16), wbr_ref[0])
         + _sigmoid(ug[:, width:2 * width]) * _dot(y_b.astype(BF16), wbr_ref[1])
         + _sigmoid(ug[:, 2 * width:3 * width]) * _dot(yc_ref[0].astype(BF16), wbr_ref[2]))
    out = _dot(m.astype(BF16), wo_ref[...])
    o_ref[0] = x_ref[0] + mod_ref[0, 0, 2:3, :] * _rms(out, g_ref[...])


def _merge(xs, u_cg, u_b, wkv_out, lru_out, y_c, mod, p, ng1, lc, skip_ctx):
    b, r, d = xs.shape
    width = y_c.shape[2]
    tm = ROW_TILE
    off = lc // tm if skip_ctx else 0
    nt = r // tm - off
    nctx = lc // tm
    heads = width // HEAD_DIM
    hid = np.arange(width) // HEAD_DIM
    e = (hid[:, None] == np.arange(V7X_LANES)[None, :]).astype(np.float32)
    e_bf, et_bf = jnp.asarray(e, BF16), jnp.asarray(e.T, BF16)
    assert heads <= V7X_LANES
    rowblk = lambda i, t: (i, t + off, 0)
    const2 = lambda i, t: (0, 0)
    act = lambda wd: pl.BlockSpec((1, tm, wd), rowblk)
    return pl.pallas_call(
        _merge_kernel,
        grid=(b, nt),
        in_specs=[
            act(d), pl.BlockSpec((1, tm, 3 * width), lambda i, t: (i, t + off, 1)),
            act(width), act(width), act(width), act(width), act(width), act(width),
            pl.BlockSpec((1, tm, width), lambda i, t: (i, t + off, 1)),
            act(width),
            pl.BlockSpec((1, 1, 6, d), lambda i, t: (i, jnp.where(t + off < nctx, 0, 1), 0, 0)),
            pl.BlockSpec((1, width), const2),
            pl.BlockSpec((1, width), const2),
            pl.BlockSpec((width, V7X_LANES), const2),
            pl.BlockSpec((V7X_LANES, width), const2),
            pl.BlockSpec((3, width, d), lambda i, t: (0, 0, 0)),
            pl.BlockSpec((d, d), const2),
            pl.BlockSpec((1, d), const2),
        ],
        out_specs=pl.BlockSpec((1, tm, d), lambda i, t: (i, t, 0)),
        out_shape=jax.ShapeDtypeStruct((b, nt * tm, d), F32),
        compiler_params=_cparams(("parallel", "parallel")),
        name="merge",
    )(xs, u_cg, wkv_out[0], wkv_out[1], wkv_out[2], wkv_out[3], lru_out[0], lru_out[1], u_b, y_c,
      mod, p["rwkv_lnx_w"].reshape(1, -1), p["rwkv_lnx_b"].reshape(1, -1), e_bf, et_bf,
      p["w_branch"].astype(BF16), p["w_out"].astype(BF16), ng1)


def _ffn_kernel(x_ref, xp_ref, xn_ref, mod_ref, g2_ref, g3_ref, wi_ref, cw_ref, cb_ref, wo_ref, o_ref,
                *, tm, nctx, nt, dff):
    t = pl.program_id(1)
    first = jnp.logical_or(t == 0, t == nctx)
    last = jnp.logical_or(t == nctx - 1, t == nt - 1)
    nb = x_ref.shape[0]

    def hidden(xv, s):
        sh, sc = mod_ref[s, 0, 3:4, :], mod_ref[s, 0, 4:5, :]
        return (_rms(xv, g2_ref[...]) * (1.0 + sc) + sh).astype(BF16)

    h = jnp.concatenate([hidden(x_ref[s], s) for s in range(nb)], axis=0)
    h_halo = jnp.concatenate([hidden(ref[s], s) for s in range(nb) for ref in (xp_ref, xn_ref)], axis=0)
    gate = _dot(h, wi_ref[:, 0:dff])
    up = _dot(h, wi_ref[:, dff:2 * dff])
    g_halo = _dot(h_halo, wi_ref[:, 0:dff])
    rows = lax.broadcasted_iota(jnp.int32, (tm, 1), 0)
    acts = []
    for s in range(nb):
        g = gate[s * tm:(s + 1) * tm]
        g_prev = g_halo[(2 * s + 1) * FFN_HALO - 1:(2 * s + 1) * FFN_HALO, :]
        g_next = g_halo[(2 * s + 1) * FFN_HALO:(2 * s + 1) * FFN_HALO + 1, :]
        g_m1 = jnp.where(rows == 0, jnp.where(first, 0.0, g_prev), pltpu.roll(g, 1, 0))
        g_p1 = jnp.where(rows == tm - 1, jnp.where(last, 0.0, g_next), pltpu.roll(g, tm - 1, 0))
        gc = cw_ref[0:1, :] * g_m1 + cw_ref[1:2, :] * g + cw_ref[2:3, :] * g_p1 + cb_ref[...]
        acts.append((gc * _sigmoid(gc) * up[s * tm:(s + 1) * tm]).astype(BF16))
    out = _dot(jnp.concatenate(acts, axis=0), wo_ref[...])
    for s in range(nb):
        o_ref[s] = x_ref[s] + mod_ref[s, 0, 5:6, :] * _rms(out[s * tm:(s + 1) * tm], g3_ref[...])


def _ffn(x1, mod, p, ng2, ng3, lc):
    b, r, d = x1.shape
    tm = ROW_TILE
    nt = r // tm
    nctx = lc // tm
    hb = tm // FFN_HALO
    nhalo = r // FFN_HALO
    dff = p["ffn_w_out"].shape[0]
    nb = FFN_BATCHES if b % FFN_BATCHES == 0 else 1
    const2 = lambda i, t: (0, 0)
    resident = pl.Buffered(1)
    return pl.pallas_call(
        functools.partial(_ffn_kernel, tm=tm, nctx=nctx, nt=nt, dff=dff),
        grid=(b // nb, nt),
        in_specs=[
            pl.BlockSpec((nb, tm, d), lambda i, t: (i, t, 0)),
            pl.BlockSpec((nb, FFN_HALO, d), lambda i, t: (i, jnp.maximum(t * hb - 1, 0), 0)),
            pl.BlockSpec((nb, FFN_HALO, d), lambda i, t: (i, jnp.minimum((t + 1) * hb, nhalo - 1), 0)),
            pl.BlockSpec((nb, 1, 6, d), lambda i, t: (i, jnp.where(t < nctx, 0, 1), 0, 0)),
            pl.BlockSpec((1, d), const2),
            pl.BlockSpec((1, d), const2),
            pl.BlockSpec((d, 2 * dff), const2, pipeline_mode=resident),
            pl.BlockSpec((3, dff), const2),
            pl.BlockSpec((1, dff), const2),
            pl.BlockSpec((dff, d), const2, pipeline_mode=resident),
        ],
        out_specs=pl.BlockSpec((nb, tm, d), lambda i, t: (i, t, 0)),
        out_shape=jax.ShapeDtypeStruct((b, r, d), F32),
        compiler_params=_cparams(("parallel", "parallel")),
        name="conv_ffn",
    )(x1, x1, x1, mod, ng2, ng3, p["ffn_w_in"].astype(BF16), p["ffn_conv_w"],
      p["ffn_conv_b"].reshape(1, -1), p["ffn_w_out"].astype(BF16))


def _in_tile(r):
    for parts in (4, 8, 2, 1, 16, 32):
        if r % parts == 0 and (r // parts) % V7X_SUBLANES == 0 and r // parts <= 1152:
            return r // parts
    raise ValueError(f"no row tile for {r} rows")


def _col_tile(n):
    for parts in (1, 2, 3, 4, 6, 8):
        if n % parts == 0 and (n // parts) % V7X_LANES == 0 and n // parts <= 1792:
            return n // parts
    raise ValueError(f"no column tile for {n} columns")


def _layer(xs, mod, p, lc, update_ctx):
    width = p["rwkv_k_k"].shape[0]
    a_cols = 3 * width + W_LORA + A_LORA + G_LORA
    bounds = np.cumsum([0, a_cols, 2 * width, 3 * width, 3 * width])
    ng = p["norm_g"]
    w_in = p["w_in"].astype(BF16)
    tm = _in_tile(xs.shape[1])
    w_a, w_b, w_c, w_g = [w_in[:, lo:hi] for lo, hi in zip(bounds[:-1], bounds[1:])]
    mu_p, mu_n = p["rwkv_mu"][0], p["rwkv_mu"][1]
    shift_taps = jnp.stack([mu_p, 1.0 - mu_p - mu_n, mu_n, jnp.zeros_like(mu_p)])
    u_a = _in_proj_taps(xs, mod, ng[0:1], w_a, shift_taps, (-1, 0, 1), lc, tm, _col_tile(a_cols))
    ident = jnp.zeros((5, width), F32).at[2].set(1.0)
    conv_taps = jnp.concatenate(
        [jnp.concatenate([p["lru_conv_w"], p["lru_conv_b"][None]], axis=0), ident], axis=1)
    u_b = _in_proj_taps(xs, mod, ng[0:1], w_b, conv_taps, (-2, -1, 0, 1), lc, tm, _col_tile(2 * width))
    u_cg = _in_proj(xs, mod, ng[0:1], jnp.concatenate([w_c, w_g], axis=1), lc, tm, _col_tile(3 * width))
    wkv_out, lru_out = _scans(u_a, u_b, p, lc)
    y_c = _na(u_cg, width, p["na_rpb"], lc, update_ctx)
    x1 = _merge(xs, u_cg, u_b, wkv_out, lru_out, y_c, mod, p, ng[1:2], lc, not update_ctx)
    return _ffn(x1, mod, p, ng[2:3], ng[3:4], lc if update_ctx else 0)


_PARAM_NAMES = ("norm_g", "w_in", "rwkv_mu", "rwkv_w0", "rwkv_w_up", "rwkv_a0", "rwkv_a_up", "rwkv_g_up",
                "rwkv_k_k", "rwkv_k_a", "rwkv_r_k", "rwkv_lnx_w", "rwkv_lnx_b", "lru_conv_w", "lru_conv_b",
                "lru_gate_a_w", "lru_gate_a_b", "lru_gate_x_w", "lru_gate_x_b", "lru_lambda", "na_rpb",
                "w_branch", "w_out", "ffn_w_in", "ffn_conv_w", "ffn_conv_b", "ffn_w_out")


def kernel(x, c, ctx, c_ctx, ada_w, ada_b, norm_g, w_in, rwkv_mu, rwkv_w0, rwkv_w_up, rwkv_a0, rwkv_a_up, rwkv_g_up, rwkv_k_k, rwkv_k_a, rwkv_r_k, rwkv_lnx_w, rwkv_lnx_b, lru_conv_w, lru_conv_b, lru_gate_a_w, lru_gate_a_b, lru_gate_x_w, lru_gate_x_b, lru_lambda, na_rpb, w_branch, w_out, ffn_w_in, ffn_conv_w, ffn_conv_b, ffn_w_out):
    stacked = dict(zip(_PARAM_NAMES, (norm_g, w_in, rwkv_mu, rwkv_w0, rwkv_w_up, rwkv_a0, rwkv_a_up,
                                      rwkv_g_up, rwkv_k_k, rwkv_k_a, rwkv_r_k, rwkv_lnx_w, rwkv_lnx_b,
                                      lru_conv_w, lru_conv_b, lru_gate_a_w, lru_gate_a_b, lru_gate_x_w,
                                      lru_gate_x_b, lru_lambda, na_rpb, w_branch, w_out, ffn_w_in,
                                      ffn_conv_w, ffn_conv_b, ffn_w_out)))
    depth = ada_w.shape[0]
    b, _, d = x.shape
    lc = ctx.shape[1]
    pad = (-(b + 1)) % V7X_SUBLANES
    cc = jnp.concatenate([c, c_ctx[None, :], jnp.zeros((pad, d), c.dtype)], axis=0)
    mod_all = _modulation(cc, ada_w, ada_b)
    xs = jnp.concatenate([ctx, x], axis=1)
    for l in range(depth):
        p = {name: val[l] for name, val in stacked.items()}
        p["rwkv_r_k"] = p["rwkv_r_k"].reshape(-1)
        m = mod_all[l].reshape(-1, 6, d)
        mod = jnp.stack([jnp.broadcast_to(m[b], (b, 6, d)), m[:b]], axis=1)
        xs = _layer(xs, mod, p, lc, l < depth - 1)
    return xs
```
